```python
import math
import jax, jax.numpy as jnp
from jax import lax
import numpy as np

D_MODEL = 2048
BATCH = 8
SEQ = 4096
DEPTH = 4
DEC_BATCH = 16
DEC_SEQ = 32
PAST_LEN = 1024

CHUNK = 64
N_MIXERS = 3
N_HEADS = 16
HEAD_DIM = D_MODEL // N_HEADS
BAND_CHUNKS = 8
BAND_PAST = BAND_CHUNKS * CHUNK
REL_CLIP_A = 256
DIFF_QK_DIM = HEAD_DIM // 2
DIFF_V_DIM = HEAD_DIM
T5_BUCKETS = 32
T5_MAX_DIST = 128
Q_BLOCK = 128
D_FF = 256 * math.ceil(8 * D_MODEL / (3 * 256))
ALPHA = (2 * DEPTH) ** 0.25
DEEPNORM_BETA = (8 * DEPTH) ** -0.25
LN_EPS = 1e-5
RMS_EPS = 1e-5
NEG_INF = -1e30
N_LAYERS_A = len(range(0, DEPTH, N_MIXERS))
N_LAYERS_B = len(range(1, DEPTH, N_MIXERS))
N_LAYERS_C = len(range(2, DEPTH, N_MIXERS))

kernel_name = 'hybrid_band_diff_stickbreak_stream_step'


def _layer_norm(x, g, b):
    xf = x.astype(jnp.float32)
    xc = xf - jnp.mean(xf, axis=-1, keepdims=True)
    var = jnp.mean(xc * xc, axis=-1, keepdims=True)
    return (xc * lax.rsqrt(var + LN_EPS) * g.astype(jnp.float32) + b.astype(jnp.float32)).astype(x.dtype)


def _swiglu(x, w_gate, w_up, w_down):
    h = jax.nn.silu(jnp.einsum('btd,df->btf', x, w_gate)) * jnp.einsum('btd,df->btf', x, w_up)
    return jnp.einsum('btf,fd->btd', h, w_down)


def _qkv(x, w_in):
    b, t, _ = x.shape
    h = jnp.einsum('btd,de->bte', x, w_in).reshape(b, t, 3, N_HEADS, HEAD_DIM)
    return h[:, :, 0], h[:, :, 1], h[:, :, 2]


def _merge_heads(o, w_out):
    b, t = o.shape[:2]
    return jnp.einsum('bte,ed->btd', o.reshape(b, t, -1), w_out)


def _masked_softmax(s, valid):
    return jax.nn.softmax(jnp.where(valid, s, NEG_INF), axis=-1)


def _chunk_causal(qpos, kpos):
    return (kpos[None, :] // CHUNK) <= (qpos[:, None] // CHUNK)


def _clipped_rel_bias(table, qpos, kpos):
    idx = jnp.clip(kpos[None, :] - qpos[:, None], -REL_CLIP_A, REL_CLIP_A) + REL_CLIP_A
    return jnp.transpose(table[idx], (2, 0, 1)).astype(jnp.float32)


def _softmax_attend(q, k, v, bias, valid):
    s = jnp.einsum('bqhd,bkhd->bhqk', q, k, preferred_element_type=jnp.float32) * (HEAD_DIM ** -0.5) + bias
    p = _masked_softmax(s, valid)
    return jnp.einsum('bhqk,bkhd->bqhd', p.astype(v.dtype), v)


def band_attn_prompt(q, k, v, table):
    b, s = q.shape[:2]
    band = BAND_PAST + CHUNK
    pad = ((0, 0), (BAND_PAST, 0), (0, 0), (0, 0))
    kp = jnp.pad(k, pad)
    vp = jnp.pad(v, pad)
    offs = jnp.arange(band) - BAND_PAST
    bias = _clipped_rel_bias(table, jnp.arange(CHUNK), offs)

    def one_chunk(c):
        start = c * CHUNK
        qc = lax.dynamic_slice_in_dim(q, start, CHUNK, axis=1)
        kb = lax.dynamic_slice_in_dim(kp, start, band, axis=1)
        vb = lax.dynamic_slice_in_dim(vp, start, band, axis=1)
        valid = jnp.broadcast_to((start + offs >= 0)[None, :], (CHUNK, band))
        return _softmax_attend(qc, kb, vb, bias, valid)

    out = lax.map(one_chunk, jnp.arange(s // CHUNK))
    return jnp.moveaxis(out, 0, 1).reshape(b, s, N_HEADS, HEAD_DIM)


def band_attn_sample(q, k_new, v_new, k_cache, v_cache, table):
    r, t = k_cache.shape[1], q.shape[1]
    k = jnp.concatenate([k_cache, k_new], axis=1)
    v = jnp.concatenate([v_cache, v_new], axis=1)
    qpos = PAST_LEN + jnp.arange(t)
    kpos = PAST_LEN - r + jnp.arange(r + t)
    qc = qpos[:, None] // CHUNK
    kc = kpos[None, :] // CHUNK
    valid = (kc <= qc) & (kc >= qc - BAND_CHUNKS)
    return _softmax_attend(q, k, v, _clipped_rel_bias(table, qpos, kpos), valid)


def _t5_bias(table, qpos, kpos):
    rel = kpos[None, :] - qpos[:, None]
    half = T5_BUCKETS // 2
    max_exact = half // 2
    n = jnp.abs(rel)
    nf = jnp.maximum(n, 1).astype(jnp.float32)
    large = max_exact + (jnp.log(nf / max_exact) / math.log(T5_MAX_DIST / max_exact) * (half - max_exact)).astype(jnp.int32)
    bucket = jnp.where(rel > 0, half, 0) + jnp.where(n < max_exact, n, jnp.minimum(large, half - 1))
    return jnp.transpose(table[bucket], (2, 0, 1)).astype(jnp.float32)


def _diff_lambda(lq1, lk1, lq2, lk2, lam_init):
    f = jnp.float32
    return (jnp.exp(jnp.sum(lq1.astype(f) * lk1.astype(f)))
            - jnp.exp(jnp.sum(lq2.astype(f) * lk2.astype(f))) + lam_init)


def _diff_attend(q, k, v, bias, valid, lam):
    b, tq = q.shape[:2]
    tk = k.shape[1]
    q2 = q.reshape(b, tq, N_HEADS, 2, DIFF_QK_DIM)
    k2 = k.reshape(b, tk, N_HEADS, 2, DIFF_QK_DIM)
    s = jnp.einsum('bqhcd,bkhcd->bchqk', q2, k2, preferred_element_type=jnp.float32) * (DIFF_QK_DIM ** -0.5) + bias
    p = _masked_softmax(s, valid)
    w = p[:, 0] - lam * p[:, 1]
    return jnp.einsum('bhqk,bkhd->bqhd', w.astype(v.dtype), v)


def _head_rmsnorm(o, g, lam_init):
    of = o.astype(jnp.float32)
    of = of * lax.rsqrt(jnp.mean(of * of, axis=-1, keepdims=True) + RMS_EPS) * g.astype(jnp.float32)
    return (of * (1.0 - lam_init)).astype(o.dtype)


def diff_attn_prompt(q, k, v, t5_table, lam):
    b, s = q.shape[:2]
    kpos = jnp.arange(s)

    def one_block(blk):
        q0 = blk * Q_BLOCK
        qb = lax.dynamic_slice_in_dim(q, q0, Q_BLOCK, axis=1)
        qpos = q0 + jnp.arange(Q_BLOCK)
        return _diff_attend(qb, k, v, _t5_bias(t5_table, qpos, kpos), _chunk_causal(qpos, kpos), lam)

    out = lax.map(one_block, jnp.arange(s // Q_BLOCK))
    return jnp.moveaxis(out, 0, 1).reshape(b, s, N_HEADS, DIFF_V_DIM)


def diff_attn_sample(q, k_new, v_new, k_cache, v_cache, t5_table, lam):
    r, t = k_cache.shape[1], q.shape[1]
    k = jnp.concatenate([k_cache, k_new], axis=1)
    v = jnp.concatenate([v_cache, v_new], axis=1)
    qpos = r + jnp.arange(t)
    kpos = jnp.arange(r + t)
    return _diff_attend(q, k, v, _t5_bias(t5_table, qpos, kpos), _chunk_causal(qpos, kpos), lam)


def _stick_breaking(q, k, v, valid):
    z = jnp.einsum('bqhd,bkhd->bhqk', q, k, preferred_element_type=jnp.float32) * (HEAD_DIM ** -0.5)
    log_beta = jax.nn.log_sigmoid(z)
    log_1m = jnp.where(valid, jax.nn.log_sigmoid(-z), 0.0)
    log_a = log_beta + lax.cumsum(log_1m, axis=3, reverse=True) - log_1m
    a = jnp.where(valid, jnp.exp(log_a), 0.0)
    return jnp.einsum('bhqk,bkhd->bqhd', a.astype(v.dtype), v)


def stick_prompt(q, k, v):
    b, s = q.shape[:2]
    kpos = jnp.arange(s)

    def one_block(blk):
        q0 = blk * Q_BLOCK
        qb = lax.dynamic_slice_in_dim(q, q0, Q_BLOCK, axis=1)
        qpos = q0 + jnp.arange(Q_BLOCK)
        return _stick_breaking(qb, k, v, kpos[None, :] < qpos[:, None])

    out = lax.map(one_block, jnp.arange(s // Q_BLOCK))
    return jnp.moveaxis(out, 0, 1).reshape(b, s, N_HEADS, HEAD_DIM)


def stick_sample(q, k_new, v_new, k_cache, v_cache):
    r, t = k_cache.shape[1], q.shape[1]
    k = jnp.concatenate([k_cache, k_new], axis=1)
    v = jnp.concatenate([v_cache, v_new], axis=1)
    qpos = r + jnp.arange(t)
    kpos = jnp.arange(r + t)
    return _stick_breaking(q, k, v, kpos[None, :] < qpos[:, None])


def setup_inputs(seed: int = 0) -> dict:
    key = jax.random.key(seed)
    k = jax.random.split(key, 28)
    f32 = jnp.float32

    def nrm(i, shape, scale):
        return jax.random.normal(k[i], shape, f32) * scale

    band_rows = min(BAND_PAST, PAST_LEN)
    d_sc = D_MODEL ** -0.5
    return {
        'x_prompt': nrm(0, (BATCH, SEQ, D_MODEL), 1.0),
        'x_sample': nrm(1, (DEC_BATCH, DEC_SEQ, D_MODEL), 1.0),
        'cache_a_k': nrm(2, (N_LAYERS_A, DEC_BATCH, band_rows, N_HEADS, HEAD_DIM), 1.0),
        'cache_a_v': nrm(3, (N_LAYERS_A, DEC_BATCH, band_rows, N_HEADS, HEAD_DIM), 1.0),
        'cache_b_k': nrm(4, (N_LAYERS_B, DEC_BATCH, PAST_LEN, N_HEADS, 2 * DIFF_QK_DIM), 1.0),
        'cache_b_v': nrm(5, (N_LAYERS_B, DEC_BATCH, PAST_LEN, N_HEADS, DIFF_V_DIM), 1.0),
        'cache_c_k': nrm(6, (N_LAYERS_C, DEC_BATCH, PAST_LEN, N_HEADS, HEAD_DIM), 1.0),
        'cache_c_v': nrm(7, (N_LAYERS_C, DEC_BATCH, PAST_LEN, N_HEADS, HEAD_DIM), 1.0),
        'w_in_a': nrm(8, (N_LAYERS_A, D_MODEL, 3 * D_MODEL), d_sc),
        'w_out_a': nrm(9, (N_LAYERS_A, D_MODEL, D_MODEL), d_sc * DEEPNORM_BETA),
        'rel_bias_a': nrm(10, (N_LAYERS_A, 2 * REL_CLIP_A + 1, N_HEADS), 0.1),
        'w_in_b': nrm(11, (N_LAYERS_B, D_MODEL, 3 * D_MODEL), d_sc),
        'w_out_b': nrm(12, (N_LAYERS_B, D_MODEL, D_MODEL), d_sc * DEEPNORM_BETA),
        'lambda_q1': nrm(13, (N_LAYERS_B, DIFF_QK_DIM), 0.1),
        'lambda_k1': nrm(14, (N_LAYERS_B, DIFF_QK_DIM), 0.1),
        'lambda_q2': nrm(15, (N_LAYERS_B, DIFF_QK_DIM), 0.1),
        'lambda_k2': nrm(16, (N_LAYERS_B, DIFF_QK_DIM), 0.1),
        'diff_norm_g': 1.0 + nrm(17, (N_LAYERS_B, DIFF_V_DIM), 0.02),
        't5_bias': nrm(18, (T5_BUCKETS, N_HEADS), 0.1),
        'w_in_c': nrm(19, (N_LAYERS_C, D_MODEL, 3 * D_MODEL), d_sc),
        'w_out_c': nrm(20, (N_LAYERS_C, D_MODEL, D_MODEL), d_sc * DEEPNORM_BETA),
        'ln1_g': 1.0 + nrm(21, (DEPTH, D_MODEL), 0.02),
        'ln1_b': nrm(22, (DEPTH, D_MODEL), 0.02),
        'ln2_g': 1.0 + nrm(23, (DEPTH, D_MODEL), 0.02),
        'ln2_b': nrm(24, (DEPTH, D_MODEL), 0.02),
        'w_gate': nrm(25, (DEPTH, D_MODEL, D_FF), d_sc),
        'w_up': nrm(26, (DEPTH, D_MODEL, D_FF), d_sc),
        'w_down': nrm(27, (DEPTH, D_FF, D_MODEL), (D_FF ** -0.5) * DEEPNORM_BETA),
    }


def reference(x_prompt, x_sample, cache_a_k, cache_a_v, cache_b_k, cache_b_v, cache_c_k, cache_c_v,
              w_in_a, w_out_a, rel_bias_a, w_in_b, w_out_b, lambda_q1, lambda_k1, lambda_q2, lambda_k2,
              diff_norm_g, t5_bias, w_in_c, w_out_c, ln1_g, ln1_b, ln2_g, ln2_b, w_gate, w_up, w_down):
    yp, ys = x_prompt, x_sample
    pa_k, pa_v, sa_k, sa_v = [], [], [], []
    pb_k, pb_v, sb_k, sb_v = [], [], [], []
    pc_k, pc_v, sc_k, sc_v = [], [], [], []
    for i in range(DEPTH):
        kind = i % N_MIXERS
        j = i // N_MIXERS
        if kind == 0:
            qp, kp, vp = _qkv(yp, w_in_a[j])
            qs, ks, vs = _qkv(ys, w_in_a[j])
            op = band_attn_prompt(qp, kp, vp, rel_bias_a[j])
            os_ = band_attn_sample(qs, ks, vs, cache_a_k[j], cache_a_v[j], rel_bias_a[j])
            keep = min(BAND_PAST, kp.shape[1])
            pa_k.append(kp[:, -keep:])
            pa_v.append(vp[:, -keep:])
            sa_k.append(ks)
            sa_v.append(vs)
            w_out = w_out_a[j]
        elif kind == 1:
            qp, kp, vp = _qkv(yp, w_in_b[j])
            qs, ks, vs = _qkv(ys, w_in_b[j])
            lam_init = 0.8 - 0.6 * math.exp(-0.3 * i)
            lam = _diff_lambda(lambda_q1[j], lambda_k1[j], lambda_q2[j], lambda_k2[j], lam_init)
            op = _head_rmsnorm(diff_attn_prompt(qp, kp, vp, t5_bias, lam), diff_norm_g[j], lam_init)
            os_ = _head_rmsnorm(diff_attn_sample(qs, ks, vs, cache_b_k[j], cache_b_v[j], t5_bias, lam),
                                diff_norm_g[j], lam_init)
            pb_k.append(kp)
            pb_v.append(vp)
            sb_k.append(ks)
            sb_v.append(vs)
            w_out = w_out_b[j]
        else:
            qp, kp, vp = _qkv(yp, w_in_c[j])
            qs, ks, vs = _qkv(ys, w_in_c[j])
            op = stick_prompt(qp, kp, vp)
            os_ = stick_sample(qs, ks, vs, cache_c_k[j], cache_c_v[j])
            pc_k.append(kp)
            pc_v.append(vp)
            sc_k.append(ks)
            sc_v.append(vs)
            w_out = w_out_c[j]
        yp = _layer_norm(ALPHA * yp + _merge_heads(op, w_out), ln1_g[i], ln1_b[i])
        ys = _layer_norm(ALPHA * ys + _merge_heads(os_, w_out), ln1_g[i], ln1_b[i])
        yp = _layer_norm(ALPHA * yp + _swiglu(yp, w_gate[i], w_up[i], w_down[i]), ln2_g[i], ln2_b[i])
        ys = _layer_norm(ALPHA * ys + _swiglu(ys, w_gate[i], w_up[i], w_down[i]), ln2_g[i], ln2_b[i])
    new_a_k_prompt = jnp.stack(pa_k)
    new_a_v_prompt = jnp.stack(pa_v)
    new_b_k_prompt = jnp.stack(pb_k)
    new_b_v_prompt = jnp.stack(pb_v)
    new_c_k_prompt = jnp.stack(pc_k)
    new_c_v_prompt = jnp.stack(pc_v)
    new_a_k_sample = jnp.stack(sa_k)
    new_a_v_sample = jnp.stack(sa_v)
    new_b_k_sample = jnp.stack(sb_k)
    new_b_v_sample = jnp.stack(sb_v)
    new_c_k_sample = jnp.stack(sc_k)
    new_c_v_sample = jnp.stack(sc_v)
    return (yp, ys, new_a_k_prompt, new_a_v_prompt, new_b_k_prompt, new_b_v_prompt, new_c_k_prompt, new_c_v_prompt,
            new_a_k_sample, new_a_v_sample, new_b_k_sample, new_b_v_sample, new_c_k_sample, new_c_v_sample)
```

```python
import functools
import math

import jax
import jax.numpy as jnp
import numpy as np
from jax import lax
from jax.experimental import pallas as pl
from jax.experimental.pallas import tpu as pltpu

F32 = jnp.float32
BF16 = jnp.bfloat16

CHUNK = 64
N_MIXERS = 3
BAND_CHUNKS = 8
BAND_PAST = BAND_CHUNKS * CHUNK
REL_CLIP_A = 256
T5_BUCKETS = 32
T5_MAX_DIST = 128
LN_EPS = 1e-5
RMS_EPS = 1e-5
NEG_INF = -1e30

LANES = 128
V7X_VMEM_LIMIT_BYTES = 56 * 1024 * 1024

_NT = (((1,), (1,)), ((), ()))


def _params(semantics):
    return pltpu.CompilerParams(dimension_semantics=semantics, vmem_limit_bytes=V7X_VMEM_LIMIT_BYTES)


def _tiles(m):
    tm = 512 if m % 512 == 0 else m
    return dict(tm=tm, tn=512, tf=512, tq=256, tk_stick=LANES)


def _layer_norm(y, g, b):
    yc = y - jnp.mean(y, axis=-1, keepdims=True)
    var = jnp.mean(yc * yc, axis=-1, keepdims=True)
    return yc * lax.rsqrt(var + LN_EPS) * g + b


def _qkv_body(x_ref, w_ref, qkv_ref, k32_ref, v32_ref, xb_ref, *, nq):
    j = pl.program_id(1)

    @pl.when(j == 0)
    def _():
        xb_ref[...] = x_ref[...].astype(BF16)

    def proj():
        return jnp.dot(xb_ref[...], w_ref[...], preferred_element_type=F32)

    @pl.when(j < nq)
    def _():
        qkv_ref[...] = proj().astype(BF16)

    @pl.when((j >= nq) & (j < 2 * nq))
    def _():
        acc = proj()
        qkv_ref[...] = acc.astype(BF16)
        k32_ref[...] = acc

    @pl.when(j >= 2 * nq)
    def _():
        acc = proj()
        qkv_ref[...] = acc.astype(BF16)
        v32_ref[...] = acc


def _qkv_proj(x, w, name):
    m, d = x.shape
    t = _tiles(m)
    tm, tn = t["tm"], t["tn"]
    nq = d // tn
    return pl.pallas_call(
        functools.partial(_qkv_body, nq=nq),
        grid=(m // tm, 3 * nq),
        in_specs=[pl.BlockSpec((tm, d), lambda i, j: (i, 0)),
                  pl.BlockSpec((d, tn), lambda i, j: (0, j))],
        out_specs=[pl.BlockSpec((tm, tn), lambda i, j: (i, j)),
                   pl.BlockSpec((tm, tn), lambda i, j: (i, jnp.clip(j - nq, 0, nq - 1))),
                   pl.BlockSpec((tm, tn), lambda i, j: (i, jnp.clip(j - 2 * nq, 0, nq - 1)))],
        out_shape=[jax.ShapeDtypeStruct((m, 3 * d), BF16),
                   jax.ShapeDtypeStruct((m, d), F32),
                   jax.ShapeDtypeStruct((m, d), F32)],
        scratch_shapes=[pltpu.VMEM((tm, d), BF16)],
        compiler_params=_params(("arbitrary", "arbitrary")),
        name=name,
    )(x, w)


def _outproj_body(o_ref, x_ref, w_ref, g_ref, b_ref, y_ref, *, alpha):
    y = alpha * x_ref[...] + jnp.dot(o_ref[...], w_ref[...], preferred_element_type=F32)
    y_ref[...] = _layer_norm(y, g_ref[...], b_ref[...])


def _outproj_ln(o, x, w, g, b, alpha, name):
    m, d = x.shape
    tm = _tiles(m)["tm"]
    row = pl.BlockSpec((tm, d), lambda i: (i, 0))
    vec = pl.BlockSpec((1, d), lambda i: (0, 0))
    return pl.pallas_call(
        functools.partial(_outproj_body, alpha=alpha),
        grid=(m // tm,),
        in_specs=[row, row, pl.BlockSpec((d, d), lambda i: (0, 0)), vec, vec],
        out_specs=row,
        out_shape=jax.ShapeDtypeStruct((m, d), F32),
        compiler_params=_params(("arbitrary",)),
        name=name,
    )(o, x, w, g.reshape(1, d), b.reshape(1, d))


def _ffn_body(x_ref, wg_ref, wu_ref, wd_ref, g_ref, b_ref, y_ref, xb_ref, acc_ref, *, alpha):
    f = pl.program_id(1)

    @pl.when(f == 0)
    def _():
        xb_ref[...] = x_ref[...].astype(BF16)
        acc_ref[...] = jnp.zeros_like(acc_ref)

    xb = xb_ref[...]
    gate = jnp.dot(xb, wg_ref[...], preferred_element_type=F32)
    up = jnp.dot(xb, wu_ref[...], preferred_element_type=F32)
    h = gate * (1.0 / (1.0 + jnp.exp(-gate))) * up
    acc_ref[...] += jnp.dot(h.astype(BF16), wd_ref[...], preferred_element_type=F32)

    @pl.when(f == pl.num_programs(1) - 1)
    def _():
        y_ref[...] = _layer_norm(alpha * x_ref[...] + acc_ref[...], g_ref[...], b_ref[...])


def _ffn_ln(x, wg, wu, wd, g, b, alpha, name):
    m, d = x.shape
    dff = wg.shape[1]
    t = _tiles(m)
    tm, tf = t["tm"], t["tf"]
    row = pl.BlockSpec((tm, d), lambda i, f: (i, 0))
    vec = pl.BlockSpec((1, d), lambda i, f: (0, 0))
    return pl.pallas_call(
        functools.partial(_ffn_body, alpha=alpha),
        grid=(m // tm, dff // tf),
        in_specs=[row,
                  pl.BlockSpec((d, tf), lambda i, f: (0, f)),
                  pl.BlockSpec((d, tf), lambda i, f: (0, f)),
                  pl.BlockSpec((tf, d), lambda i, f: (f, 0)),
                  vec, vec],
        out_specs=row,
        out_shape=jax.ShapeDtypeStruct((m, d), F32),
        scratch_shapes=[pltpu.VMEM((tm, d), BF16), pltpu.VMEM((tm, d), F32)],
        compiler_params=_params(("arbitrary", "arbitrary")),
        name=name,
    )(x, wg, wu, wd, g.reshape(1, d), b.reshape(1, d))


def _split_halves(q):
    lane = lax.broadcasted_iota(jnp.int32, q.shape, 1)
    half = q.shape[1] // 2
    zero = jnp.zeros_like(q)
    return jnp.concatenate([jnp.where(lane < half, q, zero), jnp.where(lane >= half, q, zero)], axis=0)


def _diff_finalize(o, lam_ref, g_ref, lam_init):
    t = o.shape[0] // 2
    lp = lam_ref[...]
    lam = (jnp.exp(jnp.sum(lp[0:1] * lp[1:2], axis=-1, keepdims=True))
           - jnp.exp(jnp.sum(lp[2:3] * lp[3:4], axis=-1, keepdims=True)) + lam_init)
    of = o[:t] - lam * o[t:]
    of = of * lax.rsqrt(jnp.mean(of * of, axis=-1, keepdims=True) + RMS_EPS) * g_ref[...]
    return of * (1.0 - lam_init)


def _log_sigmoid_pair(z):
    softplus_tail = jnp.log(1.0 + jnp.exp(-jnp.abs(z)))
    log_beta = jnp.minimum(z, 0.0) - softplus_tail
    return log_beta, log_beta - z


def _split_bf16(x):
    hi = x.astype(BF16)
    lo = (x - hi.astype(F32)).astype(BF16)
    return hi, lo


def _suffix_matrix(n):
    u = (np.arange(n)[:, None] > np.arange(n)[None, :]).astype(np.float32)
    blk = np.concatenate([u, np.ones((n, n), np.float32)], axis=1)
    return jnp.asarray(np.concatenate([blk, blk], axis=0), BF16)


def _attn_a_body(q_ref, k_ref, v_ref, bias_ref, o_ref, *, tq, nb, scale):
    i = pl.program_id(2)
    q = q_ref[...]
    starts, scores = [], []
    for j in range(nb):
        blk = i - (nb - 1) + j
        start = pl.multiple_of(jnp.maximum(blk, 0) * tq, tq)
        s = lax.dot_general(q, k_ref[pl.ds(start, tq), :], _NT, preferred_element_type=F32) * scale
        s = s + bias_ref[:, j * tq:(j + 1) * tq]
        if j < nb - 1:
            s = jnp.where(blk >= 0, s, NEG_INF)
        starts.append(start)
        scores.append(s)
    m = functools.reduce(jnp.maximum, [jnp.max(s, axis=-1, keepdims=True) for s in scores])
    l = jnp.zeros_like(m)
    acc = jnp.zeros((tq, v_ref.shape[1]), F32)
    for start, s in zip(starts, scores):
        p = jnp.exp(s - m)
        l = l + jnp.sum(p, axis=-1, keepdims=True)
        acc = acc + jnp.dot(p.astype(BF16), v_ref[pl.ds(start, tq), :], preferred_element_type=F32)
    o_ref[...] = (acc / l).astype(BF16)


def _band_bias_prompt(table, tq, nb):
    r = np.arange(tq)[:, None]
    off = np.arange(nb * tq)[None, :] - (nb - 1) * tq
    idx = np.clip(off - r, -REL_CLIP_A, REL_CLIP_A) + REL_CLIP_A
    kc, qc = off // CHUNK, r // CHUNK
    valid = (kc <= qc) & (kc >= qc - BAND_CHUNKS)
    bias = jnp.transpose(table[idx], (2, 0, 1)).astype(F32)
    return jnp.where(valid[None], bias, NEG_INF)


def _attn_a_prompt(qkv, table, b, t, h, hd, name):
    tq = _tiles(b * t)["tq"]
    assert BAND_PAST % tq == 0 and t % tq == 0 and tq % CHUNK == 0
    nb = BAND_PAST // tq + 1
    nt = t // tq
    bias = _band_bias_prompt(table, tq, nb)
    return pl.pallas_call(
        functools.partial(_attn_a_body, tq=tq, nb=nb, scale=hd ** -0.5),
        grid=(b, h, nt),
        in_specs=[pl.BlockSpec((tq, hd), lambda bi, hi, i: (bi * nt + i, hi)),
                  pl.BlockSpec((t, hd), lambda bi, hi, i: (bi, h + hi)),
                  pl.BlockSpec((t, hd), lambda bi, hi, i: (bi, 2 * h + hi)),
                  pl.BlockSpec((None, tq, nb * tq), lambda bi, hi, i: (hi, 0, 0))],
        out_specs=pl.BlockSpec((tq, hd), lambda bi, hi, i: (bi * nt + i, hi)),
        out_shape=jax.ShapeDtypeStruct((b * t, h * hd), BF16),
        compiler_params=_params(("arbitrary", "arbitrary", "arbitrary")),
        name=name,
    )(qkv, qkv, qkv, bias)


def _attn_b_body(q_ref, k_ref, v_ref, bias_ref, lam_ref, g_ref, o_ref, acc_ref, *, tq, scale, lam_init):
    i = pl.program_id(2)
    qq = _split_halves(q_ref[...])
    acc_ref[...] = jnp.zeros_like(acc_ref)

    def body(j, carry):
        m, l = carry
        start = pl.multiple_of(j * tq, tq)
        s = lax.dot_general(qq, k_ref[pl.ds(start, tq), :], _NT, preferred_element_type=F32) * scale
        kind = jnp.clip(j - (i - 2), 0, 2)
        s = (s.reshape(2, tq, tq) + bias_ref[kind][None]).reshape(2 * tq, tq)
        m_new = jnp.maximum(m, jnp.max(s, axis=-1, keepdims=True))
        alpha = jnp.exp(m - m_new)
        p = jnp.exp(s - m_new)
        l = alpha * l + jnp.sum(p, axis=-1, keepdims=True)
        acc_ref[...] = alpha * acc_ref[...] + jnp.dot(p.astype(BF16), v_ref[pl.ds(start, tq), :],
                                                      preferred_element_type=F32)
        return m_new, l

    m0 = jnp.full((2 * tq, 1), NEG_INF, F32)
    _, l = lax.fori_loop(0, i + 1, body, (m0, jnp.zeros_like(m0)))
    o_ref[...] = _diff_finalize(acc_ref[...] / l, lam_ref, g_ref, lam_init).astype(BF16)


def _t5_bucket(rel):
    half = T5_BUCKETS // 2
    max_exact = half // 2
    n = jnp.abs(rel)
    nf = jnp.maximum(n, 1).astype(F32)
    large = max_exact + (jnp.log(nf / max_exact) / math.log(T5_MAX_DIST / max_exact)
                         * (half - max_exact)).astype(jnp.int32)
    return jnp.where(rel > 0, half, 0) + jnp.where(n < max_exact, n, jnp.minimum(large, half - 1))


def _t5_far_distance():
    half = T5_BUCKETS // 2
    max_exact = half // 2
    return math.ceil(max_exact * (T5_MAX_DIST / max_exact) ** ((half - 1 - max_exact + 0.5) / (half - max_exact)))


def _t5_bias_prompt(table, tq):
    assert tq + 1 >= _t5_far_distance()
    r = np.arange(tq)[:, None]
    c = np.arange(tq)[None, :]
    rel = np.stack([np.full((tq, tq), -(tq + 1)), c - tq - r, c - r]).astype(np.int32)
    bias = jnp.transpose(table[_t5_bucket(jnp.asarray(rel))], (3, 0, 1, 2)).astype(F32)
    valid = np.stack([np.ones((tq, tq), bool), np.ones((tq, tq), bool), (c // CHUNK) <= (r // CHUNK)])
    return jnp.where(valid[None], bias, NEG_INF)


def _attn_b_prompt(qkv, table, lam_rows, gain, lam_init, b, t, h, hd, name):
    tq = _tiles(b * t)["tq"]
    assert t % tq == 0 and tq % CHUNK == 0
    nt = t // tq
    bias = _t5_bias_prompt(table, tq)
    return pl.pallas_call(
        functools.partial(_attn_b_body, tq=tq, scale=(hd // 2) ** -0.5, lam_init=lam_init),
        grid=(b, h, nt),
        in_specs=[pl.BlockSpec((tq, hd), lambda bi, hi, i: (bi * nt + i, hi)),
                  pl.BlockSpec((t, hd), lambda bi, hi, i: (bi, h + hi)),
                  pl.BlockSpec((t, hd), lambda bi, hi, i: (bi, 2 * h + hi)),
                  pl.BlockSpec((None, 3, tq, tq), lambda bi, hi, i: (hi, 0, 0, 0)),
                  pl.BlockSpec(lam_rows.shape, lambda bi, hi, i: (0, 0)),
                  pl.BlockSpec((1, hd), lambda bi, hi, i: (0, 0))],
        out_specs=pl.BlockSpec((tq, hd), lambda bi, hi, i: (bi * nt + i, hi)),
        out_shape=jax.ShapeDtypeStruct((b * t, h * hd), BF16),
        scratch_shapes=[pltpu.VMEM((2 * tq, hd), F32)],
        compiler_params=_params(("arbitrary", "arbitrary", "arbitrary")),
        name=name,
    )(qkv, qkv, qkv, bias, lam_rows, gain.reshape(1, hd))


def _stick_tile(q, kt, vt, u, run, valid, scale):
    tk = kt.shape[0]
    z = lax.dot_general(q, kt, _NT, preferred_element_type=F32) * scale
    log_beta, log_1m = _log_sigmoid_pair(z)
    if valid is not None:
        log_1m = jnp.where(valid, log_1m, 0.0)
    hi, lo = _split_bf16(log_1m)
    ct = jnp.dot(jnp.concatenate([hi, lo], axis=1), u, preferred_element_type=F32)
    a = jnp.exp(log_beta + ct[:, :tk] + run)
    if valid is not None:
        a = jnp.where(valid, a, 0.0)
    return jnp.dot(a.astype(BF16), vt, preferred_element_type=F32), run + ct[:, tk:]


def _attn_c_body(q_ref, k_ref, v_ref, u_ref, o_ref, acc_ref, run_ref, *, tq, tk, scale):
    i = pl.program_id(2)
    q = q_ref[...]
    nkt = (i + 1) * (tq // tk)
    qpos = lax.broadcasted_iota(jnp.int32, (tq, tk), 0) + i * tq
    koff = lax.broadcasted_iota(jnp.int32, (tq, tk), 1)
    acc_ref[...] = jnp.zeros_like(acc_ref)
    run_ref[...] = jnp.zeros_like(run_ref)

    def body(step, carry):
        start = pl.multiple_of((nkt - 1 - step) * tk, tk)
        valid = (koff + start) < qpos
        pv, run = _stick_tile(q, k_ref[pl.ds(start, tk), :], v_ref[pl.ds(start, tk), :], u_ref[...],
                              run_ref[...], valid, scale)
        acc_ref[...] += pv
        run_ref[...] = run
        return carry

    lax.fori_loop(0, nkt, body, 0)
    o_ref[...] = acc_ref[...].astype(BF16)


def _attn_c_prompt(qkv, b, t, h, hd, name):
    tl = _tiles(b * t)
    tq, tk = tl["tq"], tl["tk_stick"]
    assert t % tq == 0 and tq % tk == 0 and tk == LANES
    nt = t // tq
    return pl.pallas_call(
        functools.partial(_attn_c_body, tq=tq, tk=tk, scale=hd ** -0.5),
        grid=(b, h, nt),
        in_specs=[pl.BlockSpec((tq, hd), lambda bi, hi, i: (bi * nt + i, hi)),
                  pl.BlockSpec((t, hd), lambda bi, hi, i: (bi, h + hi)),
                  pl.BlockSpec((t, hd), lambda bi, hi, i: (bi, 2 * h + hi)),
                  pl.BlockSpec((2 * tk, 2 * tk), lambda bi, hi, i: (0, 0))],
        out_specs=pl.BlockSpec((tq, hd), lambda bi, hi, i: (bi * nt + i, hi)),
        out_shape=jax.ShapeDtypeStruct((b * t, h * hd), BF16),
        scratch_shapes=[pltpu.VMEM((tq, hd), F32), pltpu.VMEM((tq, tk), F32)],
        compiler_params=_params(("arbitrary", "arbitrary", "arbitrary")),
        name=name,
    )(qkv, qkv, qkv, _suffix_matrix(tk))


def _samp_softmax_body(q_ref, kn_ref, vn_ref, kc_ref, vc_ref, bc_ref, bn_ref, *rest, scale, diff, lam_init):
    o_ref = rest[-1]
    q = q_ref[...]
    t = q.shape[0]
    qq = _split_halves(q) if diff else q
    reps = 2 if diff else 1

    def scores(k, bias):
        s = lax.dot_general(qq, k, _NT, preferred_element_type=F32) * scale
        return (s.reshape(reps, t, -1) + bias[None]).reshape(reps * t, -1)

    s_c = scores(kc_ref[...].astype(BF16), bc_ref[...])
    s_n = scores(kn_ref[...], bn_ref[...])
    m = jnp.maximum(jnp.max(s_c, axis=-1, keepdims=True), jnp.max(s_n, axis=-1, keepdims=True))
    p_c = jnp.exp(s_c - m)
    p_n = jnp.exp(s_n - m)
    l = jnp.sum(p_c, axis=-1, keepdims=True) + jnp.sum(p_n, axis=-1, keepdims=True)
    acc = (jnp.dot(p_c.astype(BF16), vc_ref[...].astype(BF16), preferred_element_type=F32)
           + jnp.dot(p_n.astype(BF16), vn_ref[...], preferred_element_type=F32))
    o = acc / l
    if diff:
        o = _diff_finalize(o, rest[0], rest[1], lam_init)
    o_ref[...] = o.astype(BF16)


def _attn_softmax_sample(qkv, cache_k, cache_v, layer, bias, scale, bs, t, h, hd, name, diff_args=None):
    r = cache_k.shape[2]
    ck = cache_k.reshape(cache_k.shape[0], bs, r, h * hd)
    cv = cache_v.reshape(cache_v.shape[0], bs, r, h * hd)
    cache_spec = pl.BlockSpec((None, None, r, hd), lambda bi, hi: (layer, bi, 0, hi))
    in_specs = [pl.BlockSpec((t, hd), lambda bi, hi: (bi, hi)),
                pl.BlockSpec((t, hd), lambda bi, hi: (bi, h + hi)),
                pl.BlockSpec((t, hd), lambda bi, hi: (bi, 2 * h + hi)),
                cache_spec, cache_spec,
                pl.BlockSpec((None, t, r), lambda bi, hi: (hi, 0, 0)),
                pl.BlockSpec((None, t, t), lambda bi, hi: (hi, 0, 0))]
    args = [qkv, qkv, qkv, ck, cv, bias[:, :, :r], bias[:, :, r:]]
    lam_init = None
    if diff_args is not None:
        lam_rows, gain, lam_init = diff_args
        in_specs += [pl.BlockSpec(lam_rows.shape, lambda bi, hi: (0, 0)),
                     pl.BlockSpec((1, hd), lambda bi, hi: (0, 0))]
        args += [lam_rows, gain.reshape(1, hd)]
    return pl.pallas_call(
        functools.partial(_samp_softmax_body, scale=scale, diff=diff_args is not None, lam_init=lam_init),
        grid=(bs, h),
        in_specs=in_specs,
        out_specs=pl.BlockSpec((t, hd), lambda bi, hi: (bi, hi)),
        out_shape=jax.ShapeDtypeStruct((bs * t, h * hd), BF16),
        compiler_params=_params(("arbitrary", "arbitrary")),
        name=name,
    )(*args)


def _samp_stick_body(q_ref, kn_ref, vn_ref, kc_ref, vc_ref, u_ref, o_ref, *, tk, scale):
    q = q_ref[...]
    t = q.shape[0]
    z = lax.dot_general(q, kn_ref[...], _NT, preferred_element_type=F32) * scale
    log_beta, log_1m = _log_sigmoid_pair(z)
    valid = lax.broadcasted_iota(jnp.int32, (t, t), 1) < lax.broadcasted_iota(jnp.int32, (t, t), 0)
    log_1m = jnp.where(valid, log_1m, 0.0)
    suffix = (lax.broadcasted_iota(jnp.int32, (t, t), 0) > lax.broadcasted_iota(jnp.int32, (t, t), 1))
    suffix = jnp.where(suffix, 1.0, 0.0).astype(BF16)
    hi, lo = _split_bf16(log_1m)
    cum = (jnp.dot(hi, suffix, preferred_element_type=F32) + jnp.dot(lo, suffix, preferred_element_type=F32))
    a = jnp.where(valid, jnp.exp(log_beta + cum), 0.0)
    acc = jnp.dot(a.astype(BF16), vn_ref[...], preferred_element_type=F32)
    run = jnp.broadcast_to(jnp.sum(log_1m, axis=-1, keepdims=True), (t, tk))
    r = kc_ref.shape[0]
    for tile in reversed(range(r // tk)):
        kt = kc_ref[tile * tk:(tile + 1) * tk, :].astype(BF16)
        vt = vc_ref[tile * tk:(tile + 1) * tk, :].astype(BF16)
        pv, run = _stick_tile(q, kt, vt, u_ref[...], run, None, scale)
        acc = acc + pv
    o_ref[...] = acc.astype(BF16)


def _attn_stick_sample(qkv, cache_k, cache_v, layer, bs, t, h, hd, name):
    r = cache_k.shape[2]
    tk = LANES
    assert r % tk == 0
    ck = cache_k.reshape(cache_k.shape[0], bs, r, h * hd)
    cv = cache_v.reshape(cache_v.shape[0], bs, r, h * hd)
    cache_spec = pl.BlockSpec((None, None, r, hd), lambda bi, hi: (layer, bi, 0, hi))
    return pl.pallas_call(
        functools.partial(_samp_stick_body, tk=tk, scale=hd ** -0.5),
        grid=(bs, h),
        in_specs=[pl.BlockSpec((t, hd), lambda bi, hi: (bi, hi)),
                  pl.BlockSpec((t, hd), lambda bi, hi: (bi, h + hi)),
                  pl.BlockSpec((t, hd), lambda bi, hi: (bi, 2 * h + hi)),
                  cache_spec, cache_spec,
                  pl.BlockSpec((2 * tk, 2 * tk), lambda bi, hi: (0, 0))],
        out_specs=pl.BlockSpec((t, hd), lambda bi, hi: (bi, hi)),
        out_shape=jax.ShapeDtypeStruct((bs * t, h * hd), BF16),
        compiler_params=_params(("arbitrary", "arbitrary")),
        name=name,
    )(qkv, qkv, qkv, ck, cv, _suffix_matrix(tk))


def _band_bias_sample(table, past_len, r, t):
    qpos = past_len + np.arange(t)
    kpos = past_len - r + np.arange(r + t)
    idx = np.clip(kpos[None, :] - qpos[:, None], -REL_CLIP_A, REL_CLIP_A) + REL_CLIP_A
    qc, kc = qpos[:, None] // CHUNK, kpos[None, :] // CHUNK
    valid = (kc <= qc) & (kc >= qc - BAND_CHUNKS)
    bias = jnp.transpose(table[idx], (2, 0, 1)).astype(F32)
    return jnp.where(valid[None], bias, NEG_INF)


def _t5_bias_sample(table, r, t):
    qpos = r + np.arange(t)
    kpos = np.arange(r + t)
    rel = (kpos[None, :] - qpos[:, None]).astype(np.int32)
    valid = (kpos[None, :] // CHUNK) <= (qpos[:, None] // CHUNK)
    bias = jnp.transpose(table[_t5_bucket(jnp.asarray(rel))], (2, 0, 1)).astype(F32)
    return jnp.where(valid[None], bias, NEG_INF)


def kernel(x_prompt, x_sample, cache_a_k, cache_a_v, cache_b_k, cache_b_v, cache_c_k, cache_c_v, w_in_a, w_out_a, rel_bias_a, w_in_b, w_out_b, lambda_q1, lambda_k1, lambda_q2, lambda_k2, diff_norm_g, t5_bias, w_in_c, w_out_c, ln1_g, ln1_b, ln2_g, ln2_b, w_gate, w_up, w_down):
    b, t, d = x_prompt.shape
    bs, ts, _ = x_sample.shape
    h, hd = cache_a_k.shape[3], cache_a_k.shape[4]
    depth = ln1_g.shape[0]
    alpha = (2 * depth) ** 0.25
    past_len = cache_b_k.shape[2]

    yp = x_prompt.reshape(b * t, d)
    ys = x_sample.reshape(bs * ts, d)
    new = {key: [] for key in ("pa_k", "pa_v", "pb_k", "pb_v", "pc_k", "pc_v",
                               "sa_k", "sa_v", "sb_k", "sb_v", "sc_k", "sc_v")}
    for i in range(depth):
        kind, j = i % N_MIXERS, i // N_MIXERS
        w_in = (w_in_a, w_in_b, w_in_c)[kind][j].astype(BF16)
        w_out = (w_out_a, w_out_b, w_out_c)[kind][j].astype(BF16)
        qkv_p, kp, vp = _qkv_proj(yp, w_in, f"qkv_prompt_{i}")
        qkv_s, ks, vs = _qkv_proj(ys, w_in, f"qkv_sample_{i}")
        kp, vp = kp.reshape(b, t, h, hd), vp.reshape(b, t, h, hd)
        ks, vs = ks.reshape(bs, ts, h, hd), vs.reshape(bs, ts, h, hd)
        if kind == 0:
            op = _attn_a_prompt(qkv_p, rel_bias_a[j], b, t, h, hd, f"band_prompt_{i}")
            r = cache_a_k.shape[2]
            os_ = _attn_softmax_sample(qkv_s, cache_a_k, cache_a_v, j,
                                       _band_bias_sample(rel_bias_a[j], past_len, r, ts),
                                       hd ** -0.5, bs, ts, h, hd, f"band_sample_{i}")
            keep = min(BAND_PAST, t)
            kp, vp = kp[:, -keep:], vp[:, -keep:]
            tag = "a"
        elif kind == 1:
            lam_init = 0.8 - 0.6 * math.exp(-0.3 * i)
            lam_rows = jnp.stack([lambda_q1[j], lambda_k1[j], lambda_q2[j], lambda_k2[j]]).astype(F32)
            op = _attn_b_prompt(qkv_p, t5_bias, lam_rows, diff_norm_g[j], lam_init, b, t, h, hd,
                                f"diff_prompt_{i}")
            r = cache_b_k.shape[2]
            os_ = _attn_softmax_sample(qkv_s, cache_b_k, cache_b_v, j, _t5_bias_sample(t5_bias, r, ts),
                                       (hd // 2) ** -0.5, bs, ts, h, hd, f"diff_sample_{i}",
                                       diff_args=(lam_rows, diff_norm_g[j], lam_init))
            tag = "b"
        else:
            op = _attn_c_prompt(qkv_p, b, t, h, hd, f"stick_prompt_{i}")
            os_ = _attn_stick_sample(qkv_s, cache_c_k, cache_c_v, j, bs, ts, h, hd, f"stick_sample_{i}")
            tag = "c"
        new[f"p{tag}_k"].append(kp)
        new[f"p{tag}_v"].append(vp)
        new[f"s{tag}_k"].append(ks)
        new[f"s{tag}_v"].append(vs)
        yp = _outproj_ln(op, yp, w_out, ln1_g[i], ln1_b[i], alpha, f"outproj_prompt_{i}")
        ys = _outproj_ln(os_, ys, w_out, ln1_g[i], ln1_b[i], alpha, f"outproj_sample_{i}")
        wg, wu, wd = w_gate[i].astype(BF16), w_up[i].astype(BF16), w_down[i].astype(BF16)
        yp = _ffn_ln(yp, wg, wu, wd, ln2_g[i], ln2_b[i], alpha, f"ffn_prompt_{i}")
        ys = _ffn_ln(ys, wg, wu, wd, ln2_g[i], ln2_b[i], alpha, f"ffn_sample_{i}")
    return (yp.reshape(b, t, d), ys.reshape(bs, ts, d),
            jnp.stack(new["pa_k"]), jnp.stack(new["pa_v"]), jnp.stack(new["pb_k"]), jnp.stack(new["pb_v"]),
            jnp.stack(new["pc_k"]), jnp.stack(new["pc_v"]), jnp.stack(new["sa_k"]), jnp.stack(new["sa_v"]),
            jnp.stack(new["sb_k"]), jnp.stack(new["sb_v"]), jnp.stack(new["sc_k"]), jnp.stack(new["sc_v"]))
```

```python
import functools
import math

import jax
import jax.numpy as jnp
import numpy as np
from jax import lax
from jax.experimental import pallas as pl
from jax.experimental.pallas import tpu as pltpu

F32 = jnp.float32
BF16 = jnp.bfloat16

CHUNK = 64
N_MIXERS = 3
BAND_CHUNKS = 8
BAND_PAST = BAND_CHUNKS * CHUNK
REL_CLIP_A = 256
T5_BUCKETS = 32
T5_MAX_DIST = 128
LN_EPS = 1e-5
RMS_EPS = 1e-5
NEG_INF = -1e30
LOG2E = math.log2(math.e)

LANES = 128
MXU_DIM = 256
V7X_VMEM_LIMIT_BYTES = 56 * 1024 * 1024

_NT = (((1,), (1,)), ((), ()))


def _params(semantics):
    return pltpu.CompilerParams(dimension_semantics=semantics, vmem_limit_bytes=V7X_VMEM_LIMIT_BYTES)


def _tiles(m, t=None):
    tm = next(c for c in (1024, 512, m) if m % c == 0)
    cfg = dict(tm=tm, tm_ln=min(tm, 512), tn=512, tf=512)
    if t is not None:
        cfg.update(tq_band=min(256, t), band_heads=4, tq_diff=min(512, t), tq_stick=min(512, t),
                   tk_stick=MXU_DIM)
    return cfg


def _layer_norm(y, g, b):
    yc = y - jnp.mean(y, axis=-1, keepdims=True)
    var = jnp.mean(yc * yc, axis=-1, keepdims=True)
    return yc * lax.rsqrt(var + LN_EPS) * g + b


def _qkv_body(x_ref, w_ref, qkv_ref, k32_ref, v32_ref, xb_ref, *, nq, q_scale):
    j = pl.program_id(1)

    @pl.when(j == 0)
    def _():
        xb_ref[...] = x_ref[...].astype(BF16)

    def proj():
        return jnp.dot(xb_ref[...], w_ref[...], preferred_element_type=F32)

    @pl.when(j < nq)
    def _():
        qkv_ref[...] = (proj() * q_scale).astype(BF16)

    @pl.when((j >= nq) & (j < 2 * nq))
    def _():
        acc = proj()
        qkv_ref[...] = acc.astype(BF16)
        k32_ref[...] = acc

    @pl.when(j >= 2 * nq)
    def _():
        acc = proj()
        qkv_ref[...] = acc.astype(BF16)
        v32_ref[...] = acc


def _qkv_proj(x, w, q_scale, name):
    m, d = x.shape
    t = _tiles(m)
    tm, tn = t["tm"], t["tn"]
    nq = d // tn
    return pl.pallas_call(
        functools.partial(_qkv_body, nq=nq, q_scale=q_scale),
        grid=(m // tm, 3 * nq),
        in_specs=[pl.BlockSpec((tm, d), lambda i, j: (i, 0)),
                  pl.BlockSpec((d, tn), lambda i, j: (0, j))],
        out_specs=[pl.BlockSpec((tm, tn), lambda i, j: (i, j)),
                   pl.BlockSpec((tm, tn), lambda i, j: (i, jnp.clip(j - nq, 0, nq - 1))),
                   pl.BlockSpec((tm, tn), lambda i, j: (i, jnp.clip(j - 2 * nq, 0, nq - 1)))],
        out_shape=[jax.ShapeDtypeStruct((m, 3 * d), BF16),
                   jax.ShapeDtypeStruct((m, d), F32),
                   jax.ShapeDtypeStruct((m, d), F32)],
        scratch_shapes=[pltpu.VMEM((tm, d), BF16)],
        compiler_params=_params(("arbitrary", "arbitrary")),
        name=name,
    )(x, w)


def _outproj_body(o_ref, x_ref, w_ref, g_ref, b_ref, y_ref, *, alpha):
    y = alpha * x_ref[...] + jnp.dot(o_ref[...], w_ref[...], preferred_element_type=F32)
    y_ref[...] = _layer_norm(y, g_ref[...], b_ref[...])


def _outproj_ln(o, x, w, g, b, alpha, name):
    m, d = x.shape
    tm = _tiles(m)["tm_ln"]
    row = pl.BlockSpec((tm, d), lambda i: (i, 0))
    vec = pl.BlockSpec((1, d), lambda i: (0, 0))
    return pl.pallas_call(
        functools.partial(_outproj_body, alpha=alpha),
        grid=(m // tm,),
        in_specs=[row, row, pl.BlockSpec((d, d), lambda i: (0, 0)), vec, vec],
        out_specs=row,
        out_shape=jax.ShapeDtypeStruct((m, d), F32),
        compiler_params=_params(("arbitrary",)),
        name=name,
    )(o, x, w, g.reshape(1, d), b.reshape(1, d))


def _ffn_body(x_ref, wg_ref, wu_ref, wd_ref, g_ref, b_ref, y_ref, xb_ref, acc_ref, *, alpha):
    f = pl.program_id(1)

    @pl.when(f == 0)
    def _():
        xb_ref[...] = x_ref[...].astype(BF16)
        acc_ref[...] = jnp.zeros_like(acc_ref)

    xb = xb_ref[...]
    gate = jnp.dot(xb, wg_ref[...], preferred_element_type=F32)
    up = jnp.dot(xb, wu_ref[...], preferred_element_type=F32)
    h = gate * (1.0 / (1.0 + jnp.exp(-gate))) * up
    acc_ref[...] += jnp.dot(h.astype(BF16), wd_ref[...], preferred_element_type=F32)

    @pl.when(f == pl.num_programs(1) - 1)
    def _():
        y_ref[...] = _layer_norm(alpha * x_ref[...] + acc_ref[...], g_ref[...], b_ref[...])


def _ffn_ln(x, wg, wu, wd, g, b, alpha, name):
    m, d = x.shape
    dff = wg.shape[1]
    t = _tiles(m)
    tm, tf = t["tm_ln"], t["tf"]
    row = pl.BlockSpec((tm, d), lambda i, f: (i, 0))
    vec = pl.BlockSpec((1, d), lambda i, f: (0, 0))
    return pl.pallas_call(
        functools.partial(_ffn_body, alpha=alpha),
        grid=(m // tm, dff // tf),
        in_specs=[row,
                  pl.BlockSpec((d, tf), lambda i, f: (0, f)),
                  pl.BlockSpec((d, tf), lambda i, f: (0, f)),
                  pl.BlockSpec((tf, d), lambda i, f: (f, 0)),
                  vec, vec],
        out_specs=row,
        out_shape=jax.ShapeDtypeStruct((m, d), F32),
        scratch_shapes=[pltpu.VMEM((tm, d), BF16), pltpu.VMEM((tm, d), F32)],
        compiler_params=_params(("arbitrary", "arbitrary")),
        name=name,
    )(x, wg, wu, wd, g.reshape(1, d), b.reshape(1, d))


def _toeplitz(vec, rows, cols):
    hh, length = vec.shape
    assert length == rows + cols - 1
    padded = jnp.pad(vec, ((0, 0), (0, 1)))
    skew = jnp.tile(padded, (1, rows))[:, :rows * length].reshape(hh, rows, length)
    return skew[:, :, rows - 1:]


def _split_halves(q):
    lane = lax.broadcasted_iota(jnp.int32, q.shape, 1)
    half = q.shape[1] // 2
    zero = jnp.zeros_like(q)
    return jnp.concatenate([jnp.where(lane < half, q, zero), jnp.where(lane >= half, q, zero)], axis=0)


def _diff_finalize(o, lam_ref, g_ref, lam_init):
    t = o.shape[0] // 2
    lp = lam_ref[...]
    lam = (jnp.exp(jnp.sum(lp[0:1] * lp[1:2], axis=-1, keepdims=True))
           - jnp.exp(jnp.sum(lp[2:3] * lp[3:4], axis=-1, keepdims=True)) + lam_init)
    of = o[:t] - lam * o[t:]
    of = of * lax.rsqrt(jnp.mean(of * of, axis=-1, keepdims=True) + RMS_EPS) * g_ref[...]
    return of * (1.0 - lam_init)


def _neg_abs(z):
    return pltpu.bitcast(pltpu.bitcast(z, jnp.uint32) | jnp.uint32(0x80000000), F32)


def _suffix_matrix(n):
    return jnp.asarray(np.arange(n)[:, None] > np.arange(n)[None, :], BF16)


def _stick_tile(q, kt, vt, u, run, valid):
    z = lax.dot_general(q, kt, _NT, preferred_element_type=F32)
    log_beta = jnp.minimum(z, 0.0) - jnp.log(1.0 + jnp.exp2(_neg_abs(z))) * LOG2E
    log_1m = log_beta - z
    if valid is not None:
        log_1m = jnp.where(valid, log_1m, 0.0)
    suffix = jnp.dot(log_1m.astype(BF16), u, preferred_element_type=F32)
    a = jnp.exp2(log_beta + suffix + run)
    if valid is not None:
        a = jnp.where(valid, a, 0.0)
    return (jnp.dot(a.astype(BF16), vt, preferred_element_type=F32),
            run + jnp.sum(log_1m, axis=-1, keepdims=True))


def _attn_a_body(q_ref, k_ref, v_ref, bias_ref, o_ref, *, tq, nb, heads, hd):
    i = pl.program_id(2)
    starts = [pl.multiple_of(jnp.maximum(i - (nb - 1) + j, 0) * tq, tq) for j in range(nb)]
    for hh in range(heads):
        cols = slice(hh * hd, (hh + 1) * hd)
        q = q_ref[:, cols]
        scores = []
        for j in range(nb):
            s = lax.dot_general(q, k_ref[pl.ds(starts[j], tq), cols], _NT, preferred_element_type=F32)
            s = s + bias_ref[hh, :, j * tq:(j + 1) * tq]
            if j < nb - 1:
                s = jnp.where(i - (nb - 1) + j >= 0, s, NEG_INF)
            scores.append(s)
        m = functools.reduce(jnp.maximum, [jnp.max(s, axis=-1, keepdims=True) for s in scores])
        l = jnp.zeros_like(m)
        acc = jnp.zeros((tq, hd), F32)
        for start, s in zip(starts, scores):
            p = jnp.exp2(s - m)
            l = l + jnp.sum(p, axis=-1, keepdims=True)
            acc = acc + jnp.dot(p.astype(BF16), v_ref[pl.ds(start, tq), cols], preferred_element_type=F32)
        o_ref[:, cols] = (acc / l).astype(BF16)


def _clipped_bias_vec(table, rel):
    idx = np.clip(rel, -REL_CLIP_A, REL_CLIP_A) + REL_CLIP_A
    return jnp.transpose(table[idx]).astype(F32) * LOG2E


def _band_bias_prompt(table, tq, nb):
    w = nb * tq
    r = np.arange(tq)[:, None]
    off = np.arange(w)[None, :] - (nb - 1) * tq
    kc, qc = off // CHUNK, r // CHUNK
    valid = (kc <= qc) & (kc >= qc - BAND_CHUNKS)
    vec = _clipped_bias_vec(table, np.arange(tq + w - 1) - (tq - 1) - (nb - 1) * tq)
    return jnp.where(valid[None], _toeplitz(vec, tq, w), NEG_INF)


def _attn_a_prompt(qkv, table, b, t, h, hd, name):
    cfg = _tiles(b * t, t)
    tq, heads = cfg["tq_band"], cfg["band_heads"]
    assert BAND_PAST % tq == 0 and t % tq == 0 and tq % CHUNK == 0 and h % heads == 0
    nb = BAND_PAST // tq + 1
    nt = t // tq
    hg = h // heads
    bias = _band_bias_prompt(table, tq, nb)
    return pl.pallas_call(
        functools.partial(_attn_a_body, tq=tq, nb=nb, heads=heads, hd=hd),
        grid=(b, hg, nt),
        in_specs=[pl.BlockSpec((tq, heads * hd), lambda bi, gi, i: (bi * nt + i, gi)),
                  pl.BlockSpec((t, heads * hd), lambda bi, gi, i: (bi, hg + gi)),
                  pl.BlockSpec((t, heads * hd), lambda bi, gi, i: (bi, 2 * hg + gi)),
                  pl.BlockSpec((heads, tq, nb * tq), lambda bi, gi, i: (gi, 0, 0))],
        out_specs=pl.BlockSpec((tq, heads * hd), lambda bi, gi, i: (bi * nt + i, gi)),
        out_shape=jax.ShapeDtypeStruct((b * t, h * hd), BF16),
        compiler_params=_params(("arbitrary", "arbitrary", "arbitrary")),
        name=name,
    )(qkv, qkv, qkv, bias)


def _lane_fold(x, op):
    return functools.reduce(op, [x[:, c * LANES:(c + 1) * LANES] for c in range(x.shape[1] // LANES)])


def _attn_b_body(q_ref, k_ref, v_ref, near_ref, far_ref, lam_ref, g_ref, o_ref, acc_ref, mx_ref, l_ref, *,
                 tq, lam_init):
    i = pl.program_id(2)
    qq = _split_halves(q_ref[...])
    far = far_ref[:, 0:1]
    n_far = jnp.maximum(i - 1, 0)

    def far_scores(j):
        start = pl.multiple_of(j * tq, tq)
        return lax.dot_general(qq, k_ref[pl.ds(start, tq), :], _NT, preferred_element_type=F32), start

    def near_scores(j):
        s, start = far_scores(j)
        return (s.reshape(2, tq, tq) + near_ref[j - i + 1][None]).reshape(2 * tq, tq), start

    def max_pass(scores, lo, hi):
        mx_ref[...] = jnp.full_like(mx_ref, NEG_INF)

        def body(j, carry):
            mx_ref[...] = jnp.maximum(mx_ref[...], _lane_fold(scores(j)[0], jnp.maximum))
            return carry

        lax.fori_loop(lo, hi, body, 0)
        return jnp.max(mx_ref[...], axis=-1, keepdims=True)

    m = jnp.maximum(max_pass(far_scores, 0, n_far) + far, max_pass(near_scores, n_far, i + 1))
    mx_ref[...] = jnp.broadcast_to(m, mx_ref.shape)
    l_ref[...] = jnp.zeros_like(l_ref)
    acc_ref[...] = jnp.zeros_like(acc_ref)

    def sum_pass(scores, shift, lo, hi):
        def body(j, carry):
            s, start = scores(j)
            p = jnp.exp2(s - jnp.tile(mx_ref[...] - shift, (1, tq // LANES)))
            l_ref[...] += _lane_fold(p, jnp.add)
            acc_ref[...] += jnp.dot(p.astype(BF16), v_ref[pl.ds(start, tq), :], preferred_element_type=F32)
            return carry

        lax.fori_loop(lo, hi, body, 0)

    sum_pass(far_scores, far, 0, n_far)
    sum_pass(near_scores, 0.0, n_far, i + 1)
    l = jnp.sum(l_ref[...], axis=-1, keepdims=True)
    o_ref[...] = _diff_finalize(acc_ref[...] / l, lam_ref, g_ref, lam_init).astype(BF16)


def _t5_bucket(rel):
    half = T5_BUCKETS // 2
    max_exact = half // 2
    n = jnp.abs(rel)
    nf = jnp.maximum(n, 1).astype(F32)
    large = max_exact + (jnp.log(nf / max_exact) / math.log(T5_MAX_DIST / max_exact)
                         * (half - max_exact)).astype(jnp.int32)
    return jnp.where(rel > 0, half, 0) + jnp.where(n < max_exact, n, jnp.minimum(large, half - 1))


def _t5_far_distance():
    half = T5_BUCKETS // 2
    max_exact = half // 2
    return math.ceil(max_exact * (T5_MAX_DIST / max_exact) ** ((half - 1 - max_exact + 0.5) / (half - max_exact)))


def _t5_bias_vec(table, rel):
    return jnp.transpose(table[_t5_bucket(jnp.asarray(rel, jnp.int32))]).astype(F32) * LOG2E


def _t5_bias_prompt(table, tq):
    assert tq + 1 >= _t5_far_distance()
    r = np.arange(tq)[:, None]
    c = np.arange(tq)[None, :]
    left = _toeplitz(_t5_bias_vec(table, np.arange(2 * tq - 1) - (tq - 1) - tq), tq, tq)
    diag = _toeplitz(_t5_bias_vec(table, np.arange(2 * tq - 1) - (tq - 1)), tq, tq)
    diag = jnp.where(((c // CHUNK) <= (r // CHUNK))[None], diag, NEG_INF)
    far = jnp.broadcast_to(_t5_bias_vec(table, np.array([-(tq + 1)]))[:, :, None], (table.shape[1], 1, LANES))
    return jnp.stack([left, diag], axis=1), far


def _attn_b_prompt(qkv, table, lam_rows, gain, lam_init, b, t, h, hd, name):
    tq = _tiles(b * t, t)["tq_diff"]
    assert t % tq == 0 and tq % CHUNK == 0
    nt = t // tq
    near, far = _t5_bias_prompt(table, tq)
    return pl.pallas_call(
        functools.partial(_attn_b_body, tq=tq, lam_init=lam_init),
        grid=(b, h, nt),
        in_specs=[pl.BlockSpec((tq, hd), lambda bi, hi, i: (bi * nt + i, hi)),
                  pl.BlockSpec((t, hd), lambda bi, hi, i: (bi, h + hi)),
                  pl.BlockSpec((t, hd), lambda bi, hi, i: (bi, 2 * h + hi)),
                  pl.BlockSpec((None, 2, tq, tq), lambda bi, hi, i: (hi, 0, 0, 0)),
                  pl.BlockSpec((None, 1, LANES), lambda bi, hi, i: (hi, 0, 0)),
                  pl.BlockSpec(lam_rows.shape, lambda bi, hi, i: (0, 0)),
                  pl.BlockSpec((1, hd), lambda bi, hi, i: (0, 0))],
        out_specs=pl.BlockSpec((tq, hd), lambda bi, hi, i: (bi * nt + i, hi)),
        out_shape=jax.ShapeDtypeStruct((b * t, h * hd), BF16),
        scratch_shapes=[pltpu.VMEM((2 * tq, hd), F32), pltpu.VMEM((2 * tq, LANES), F32),
                        pltpu.VMEM((2 * tq, LANES), F32)],
        compiler_params=_params(("arbitrary", "arbitrary", "arbitrary")),
        name=name,
    )(qkv, qkv, qkv, near, far, lam_rows, gain.reshape(1, hd))


def _attn_c_body(q_ref, k_ref, v_ref, u_ref, o_ref, acc_ref, *, tq, tk):
    i = pl.program_id(2)
    q = q_ref[...]
    u = u_ref[...]
    subs = tq // tk
    row = lax.broadcasted_iota(jnp.int32, (tq, tk), 0)
    col = lax.broadcasted_iota(jnp.int32, (tq, tk), 1)

    def group(base, run, masked):
        total = None
        for sub in reversed(range(subs)):
            start = pl.multiple_of(base + sub * tk, tk)
            valid = (col + sub * tk < row) if masked else None
            pv, run = _stick_tile(q, k_ref[pl.ds(start, tk), :], v_ref[pl.ds(start, tk), :], u, run, valid)
            total = pv if total is None else total + pv
        return total, run

    pv, run = group(i * tq, jnp.zeros((tq, 1), F32), True)
    acc_ref[...] = pv

    def body(step, run):
        pv, run = group((i - 1 - step) * tq, run, False)
        acc_ref[...] += pv
        return run

    lax.fori_loop(0, i, body, run)
    o_ref[...] = acc_ref[...].astype(BF16)


def _attn_c_prompt(qkv, b, t, h, hd, name):
    cfg = _tiles(b * t, t)
    tq, tk = cfg["tq_stick"], min(cfg["tk_stick"], cfg["tq_stick"])
    assert t % tq == 0 and tq % tk == 0
    nt = t // tq
    return pl.pallas_call(
        functools.partial(_attn_c_body, tq=tq, tk=tk),
        grid=(b, h, nt),
        in_specs=[pl.BlockSpec((tq, hd), lambda bi, hi, i: (bi * nt + i, hi)),
                  pl.BlockSpec((t, hd), lambda bi, hi, i: (bi, h + hi)),
                  pl.BlockSpec((t, hd), lambda bi, hi, i: (bi, 2 * h + hi)),
                  pl.BlockSpec((tk, tk), lambda bi, hi, i: (0, 0))],
        out_specs=pl.BlockSpec((tq, hd), lambda bi, hi, i: (bi * nt + i, hi)),
        out_shape=jax.ShapeDtypeStruct((b * t, h * hd), BF16),
        scratch_shapes=[pltpu.VMEM((tq, hd), F32)],
        compiler_params=_params(("arbitrary", "arbitrary", "arbitrary")),
        name=name,
    )(qkv, qkv, qkv, _suffix_matrix(tk))


def _samp_softmax_body(q_ref, kn_ref, vn_ref, kc_ref, vc_ref, bc_ref, bn_ref, *rest, diff, lam_init):
    o_ref = rest[-1]
    q = q_ref[...]
    t = q.shape[0]
    qq = _split_halves(q) if diff else q
    reps = 2 if diff else 1

    def scores(k, bias):
        s = lax.dot_general(qq, k, _NT, preferred_element_type=F32)
        return (s.reshape(reps, t, -1) + bias[None]).reshape(reps * t, -1)

    s_c = scores(kc_ref[...].astype(BF16), bc_ref[...])
    s_n = scores(kn_ref[...], bn_ref[...])
    m = jnp.maximum(jnp.max(s_c, axis=-1, keepdims=True), jnp.max(s_n, axis=-1, keepdims=True))
    p_c = jnp.exp2(s_c - m)
    p_n = jnp.exp2(s_n - m)
    l = jnp.sum(p_c, axis=-1, keepdims=True) + jnp.sum(p_n, axis=-1, keepdims=True)
    acc = (jnp.dot(p_c.astype(BF16), vc_ref[...].astype(BF16), preferred_element_type=F32)
           + jnp.dot(p_n.astype(BF16), vn_ref[...], preferred_element_type=F32))
    o = acc / l
    if diff:
        o = _diff_finalize(o, rest[0], rest[1], lam_init)
    o_ref[...] = o.astype(BF16)


def _attn_softmax_sample(qkv, cache_k, cache_v, layer, bias, bs, t, h, hd, name, diff_args=None):
    r = cache_k.shape[2]
    ck = cache_k.reshape(cache_k.shape[0], bs, r, h * hd)
    cv = cache_v.reshape(cache_v.shape[0], bs, r, h * hd)
    cache_spec = pl.BlockSpec((None, None, r, hd), lambda bi, hi: (layer, bi, 0, hi))
    in_specs = [pl.BlockSpec((t, hd), lambda bi, hi: (bi, hi)),
                pl.BlockSpec((t, hd), lambda bi, hi: (bi, h + hi)),
                pl.BlockSpec((t, hd), lambda bi, hi: (bi, 2 * h + hi)),
                cache_spec, cache_spec,
                pl.BlockSpec((None, t, r), lambda bi, hi: (hi, 0, 0)),
                pl.BlockSpec((None, t, t), lambda bi, hi: (hi, 0, 0))]
    args = [qkv, qkv, qkv, ck, cv, bias[:, :, :r], bias[:, :, r:]]
    lam_init = None
    if diff_args is not None:
        lam_rows, gain, lam_init = diff_args
        in_specs += [pl.BlockSpec(lam_rows.shape, lambda bi, hi: (0, 0)),
                     pl.BlockSpec((1, hd), lambda bi, hi: (0, 0))]
        args += [lam_rows, gain.reshape(1, hd)]
    return pl.pallas_call(
        functools.partial(_samp_softmax_body, diff=diff_args is not None, lam_init=lam_init),
        grid=(bs, h),
        in_specs=in_specs,
        out_specs=pl.BlockSpec((t, hd), lambda bi, hi: (bi, hi)),
        out_shape=jax.ShapeDtypeStruct((bs * t, h * hd), BF16),
        compiler_params=_params(("arbitrary", "arbitrary")),
        name=name,
    )(*args)


def _samp_stick_body(q_ref, kn_ref, vn_ref, kc_ref, vc_ref, un_ref, u_ref, o_ref, *, tk):
    q = q_ref[...]
    t = q.shape[0]
    valid = lax.broadcasted_iota(jnp.int32, (t, t), 1) < lax.broadcasted_iota(jnp.int32, (t, t), 0)
    acc, run = _stick_tile(q, kn_ref[...], vn_ref[...], un_ref[...], jnp.zeros((t, 1), F32), valid)
    for tile in reversed(range(kc_ref.shape[0] // tk)):
        rows = slice(tile * tk, (tile + 1) * tk)
        pv, run = _stick_tile(q, kc_ref[rows, :].astype(BF16), vc_ref[rows, :].astype(BF16), u_ref[...],
                              run, None)
        acc = acc + pv
    o_ref[...] = acc.astype(BF16)


def _attn_stick_sample(qkv, cache_k, cache_v, layer, bs, t, h, hd, name):
    r = cache_k.shape[2]
    tk = MXU_DIM if r % MXU_DIM == 0 else LANES
    assert r % tk == 0
    ck = cache_k.reshape(cache_k.shape[0], bs, r, h * hd)
    cv = cache_v.reshape(cache_v.shape[0], bs, r, h * hd)
    cache_spec = pl.BlockSpec((None, None, r, hd), lambda bi, hi: (layer, bi, 0, hi))
    return pl.pallas_call(
        functools.partial(_samp_stick_body, tk=tk),
        grid=(bs, h),
        in_specs=[pl.BlockSpec((t, hd), lambda bi, hi: (bi, hi)),
                  pl.BlockSpec((t, hd), lambda bi, hi: (bi, h + hi)),
                  pl.BlockSpec((t, hd), lambda bi, hi: (bi, 2 * h + hi)),
                  cache_spec, cache_spec,
                  pl.BlockSpec((t, t), lambda bi, hi: (0, 0)),
                  pl.BlockSpec((tk, tk), lambda bi, hi: (0, 0))],
        out_specs=pl.BlockSpec((t, hd), lambda bi, hi: (bi, hi)),
        out_shape=jax.ShapeDtypeStruct((bs * t, h * hd), BF16),
        compiler_params=_params(("arbitrary", "arbitrary")),
        name=name,
    )(qkv, qkv, qkv, ck, cv, _suffix_matrix(t), _suffix_matrix(tk))


def _band_bias_sample(table, past_len, r, t):
    qpos = past_len + np.arange(t)
    kpos = past_len - r + np.arange(r + t)
    qc, kc = qpos[:, None] // CHUNK, kpos[None, :] // CHUNK
    valid = (kc <= qc) & (kc >= qc - BAND_CHUNKS)
    vec = _clipped_bias_vec(table, np.arange(r + 2 * t - 1) - (t - 1) + (kpos[0] - qpos[0]))
    return jnp.where(valid[None], _toeplitz(vec, t, r + t), NEG_INF)


def _t5_bias_sample(table, r, t):
    qpos = r + np.arange(t)
    kpos = np.arange(r + t)
    valid = (kpos[None, :] // CHUNK) <= (qpos[:, None] // CHUNK)
    vec = _t5_bias_vec(table, np.arange(r + 2 * t - 1) - (t - 1) + (kpos[0] - qpos[0]))
    return jnp.where(valid[None], _toeplitz(vec, t, r + t), NEG_INF)


def kernel(x_prompt, x_sample, cache_a_k, cache_a_v, cache_b_k, cache_b_v, cache_c_k, cache_c_v, w_in_a, w_out_a, rel_bias_a, w_in_b, w_out_b, lambda_q1, lambda_k1, lambda_q2, lambda_k2, diff_norm_g, t5_bias, w_in_c, w_out_c, ln1_g, ln1_b, ln2_g, ln2_b, w_gate, w_up, w_down):
    b, t, d = x_prompt.shape
    bs, ts, _ = x_sample.shape
    h, hd = cache_a_k.shape[3], cache_a_k.shape[4]
    depth = ln1_g.shape[0]
    alpha = (2 * depth) ** 0.25
    past_len = cache_b_k.shape[2]

    yp = x_prompt.reshape(b * t, d)
    ys = x_sample.reshape(bs * ts, d)
    new = {key: [] for key in ("pa_k", "pa_v", "pb_k", "pb_v", "pc_k", "pc_v",
                               "sa_k", "sa_v", "sb_k", "sb_v", "sc_k", "sc_v")}
    for i in range(depth):
        kind, j = i % N_MIXERS, i // N_MIXERS
        w_in = (w_in_a, w_in_b, w_in_c)[kind][j].astype(BF16)
        w_out = (w_out_a, w_out_b, w_out_c)[kind][j].astype(BF16)
        q_scale = ((hd // 2) ** -0.5 if kind == 1 else hd ** -0.5) * LOG2E
        qkv_p, kp, vp = _qkv_proj(yp, w_in, q_scale, f"qkv_prompt_{i}")
        qkv_s, ks, vs = _qkv_proj(ys, w_in, q_scale, f"qkv_sample_{i}")
        kp, vp = kp.reshape(b, t, h, hd), vp.reshape(b, t, h, hd)
        ks, vs = ks.reshape(bs, ts, h, hd), vs.reshape(bs, ts, h, hd)
        if kind == 0:
            op = _attn_a_prompt(qkv_p, rel_bias_a[j], b, t, h, hd, f"band_prompt_{i}")
            r = cache_a_k.shape[2]
            os_ = _attn_softmax_sample(qkv_s, cache_a_k, cache_a_v, j,
                                       _band_bias_sample(rel_bias_a[j], past_len, r, ts),
                                       bs, ts, h, hd, f"band_sample_{i}")
            keep = min(BAND_PAST, t)
            kp, vp = kp[:, -keep:], vp[:, -keep:]
            tag = "a"
        elif kind == 1:
            lam_init = 0.8 - 0.6 * math.exp(-0.3 * i)
            lam_rows = jnp.stack([lambda_q1[j], lambda_k1[j], lambda_q2[j], lambda_k2[j]]).astype(F32)
            op = _attn_b_prompt(qkv_p, t5_bias, lam_rows, diff_norm_g[j], lam_init, b, t, h, hd,
                                f"diff_prompt_{i}")
            r = cache_b_k.shape[2]
            os_ = _attn_softmax_sample(qkv_s, cache_b_k, cache_b_v, j, _t5_bias_sample(t5_bias, r, ts),
                                       bs, ts, h, hd, f"diff_sample_{i}",
                                       diff_args=(lam_rows, diff_norm_g[j], lam_init))
            tag = "b"
        else:
            op = _attn_c_prompt(qkv_p, b, t, h, hd, f"stick_prompt_{i}")
            os_ = _attn_stick_sample(qkv_s, cache_c_k, cache_c_v, j, bs, ts, h, hd, f"stick_sample_{i}")
            tag = "c"
        new[f"p{tag}_k"].append(kp)
        new[f"p{tag}_v"].append(vp)
        new[f"s{tag}_k"].append(ks)
        new[f"s{tag}_v"].append(vs)
        yp = _outproj_ln(op, yp, w_out, ln1_g[i], ln1_b[i], alpha, f"outproj_prompt_{i}")
        ys = _outproj_ln(os_, ys, w_out, ln1_g[i], ln1_b[i], alpha, f"outproj_sample_{i}")
        wg, wu, wd = w_gate[i].astype(BF16), w_up[i].astype(BF16), w_down[i].astype(BF16)
        yp = _ffn_ln(yp, wg, wu, wd, ln2_g[i], ln2_b[i], alpha, f"ffn_prompt_{i}")
        ys = _ffn_ln(ys, wg, wu, wd, ln2_g[i], ln2_b[i], alpha, f"ffn_sample_{i}")
    return (yp.reshape(b, t, d), ys.reshape(bs, ts, d),
            jnp.stack(new["pa_k"]), jnp.stack(new["pa_v"]), jnp.stack(new["pb_k"]), jnp.stack(new["pb_v"]),
            jnp.stack(new["pc_k"]), jnp.stack(new["pc_v"]), jnp.stack(new["sa_k"]), jnp.stack(new["sa_v"]),
            jnp.stack(new["sb_k"]), jnp.stack(new["sb_v"]), jnp.stack(new["sc_k"]), jnp.stack(new["sc_v"]))
```

```python
import functools
import math

import jax
import jax.numpy as jnp
import numpy as np
from jax import lax
from jax.experimental import pallas as pl
from jax.experimental.pallas import tpu as pltpu

F32 = jnp.float32
BF16 = jnp.bfloat16

CHUNK = 64
N_MIXERS = 3
BAND_CHUNKS = 8
BAND_PAST = BAND_CHUNKS * CHUNK
REL_CLIP_A = 256
T5_BUCKETS = 32
T5_MAX_DIST = 128
LN_EPS = 1e-5
RMS_EPS = 1e-5
NEG_INF = -1e30
LOG2E = math.log2(math.e)

LANES = 128
SUBLANES = 8
MXU_DIM = 256
V7X_VMEM_LIMIT_BYTES = 56 * 1024 * 1024

_NT = (((1,), (1,)), ((), ()))


def _params(semantics):
    return pltpu.CompilerParams(dimension_semantics=semantics, vmem_limit_bytes=V7X_VMEM_LIMIT_BYTES)


def _tiles(m, t=None):
    tm = next(c for c in (1024, 512, m) if m % c == 0)
    cfg = dict(tm=tm, tm_ln=min(tm, 512), tf=512)
    if t is not None:
        cfg.update(tq_band=min(256, t), band_heads=4, tq_diff=min(512, t), tq_stick=min(512, t),
                   tk_stick=MXU_DIM)
    return cfg


def _layer_norm(y, g, b):
    yc = y - jnp.mean(y, axis=-1, keepdims=True)
    var = jnp.mean(yc * yc, axis=-1, keepdims=True)
    return yc * lax.rsqrt(var + LN_EPS) * g + b


def _qkv_body(x_ref, w_ref, qkv_ref, k32_ref, v32_ref, *, nq, hd, q_scale, head_major):
    j = pl.program_id(1)

    def proj():
        return jnp.dot(x_ref[...], w_ref[...], preferred_element_type=F32)

    def store_f32(ref, acc):
        if head_major:
            for hh in range(SUBLANES):
                ref[:, hh, :] = acc[:, hh * hd:(hh + 1) * hd]
        else:
            ref[...] = acc

    @pl.when(j < nq)
    def _():
        qkv_ref[...] = (proj() * q_scale).astype(BF16)

    @pl.when((j >= nq) & (j < 2 * nq))
    def _():
        acc = proj()
        qkv_ref[...] = acc.astype(BF16)
        store_f32(k32_ref, acc)

    @pl.when(j >= 2 * nq)
    def _():
        acc = proj()
        qkv_ref[...] = acc.astype(BF16)
        store_f32(v32_ref, acc)


def _qkv_proj(x, w, q_scale, hd, head_major, name):
    m, d = x.shape
    tm = _tiles(m)["tm"]
    tn = SUBLANES * hd
    nq = d // tn

    def f32_spec(first):
        if head_major:
            return pl.BlockSpec((tm, None, SUBLANES, hd), lambda i, j: (i, jnp.clip(j - first, 0, nq - 1), 0, 0))
        return pl.BlockSpec((tm, tn), lambda i, j: (i, jnp.clip(j - first, 0, nq - 1)))

    f32_shape = jax.ShapeDtypeStruct((m, nq, SUBLANES, hd) if head_major else (m, d), F32)
    qkv, k32, v32 = pl.pallas_call(
        functools.partial(_qkv_body, nq=nq, hd=hd, q_scale=q_scale, head_major=head_major),
        grid=(m // tm, 3 * nq),
        in_specs=[pl.BlockSpec((tm, d), lambda i, j: (i, 0)),
                  pl.BlockSpec((d, tn), lambda i, j: (0, j))],
        out_specs=[pl.BlockSpec((tm, tn), lambda i, j: (i, j)), f32_spec(nq), f32_spec(2 * nq)],
        out_shape=[jax.ShapeDtypeStruct((m, 3 * d), BF16), f32_shape, f32_shape],
        compiler_params=_params(("arbitrary", "arbitrary")),
        name=name,
    )(x, w)
    return qkv, k32.reshape(m, nq * SUBLANES, hd), v32.reshape(m, nq * SUBLANES, hd)


def _outproj_body(o_ref, x_ref, w_ref, g_ref, b_ref, y_ref, *, alpha):
    y = alpha * x_ref[...] + jnp.dot(o_ref[...], w_ref[...], preferred_element_type=F32)
    y_ref[...] = _layer_norm(y, g_ref[...], b_ref[...])


def _outproj_ln(o, x, w, g, b, alpha, name):
    m, d = x.shape
    tm = _tiles(m)["tm_ln"]
    row = pl.BlockSpec((tm, d), lambda i: (i, 0))
    vec = pl.BlockSpec((1, d), lambda i: (0, 0))
    return pl.pallas_call(
        functools.partial(_outproj_body, alpha=alpha),
        grid=(m // tm,),
        in_specs=[row, row, pl.BlockSpec((d, d), lambda i: (0, 0)), vec, vec],
        out_specs=row,
        out_shape=jax.ShapeDtypeStruct((m, d), F32),
        compiler_params=_params(("arbitrary",)),
        name=name,
    )(o, x, w, g.reshape(1, d), b.reshape(1, d))


def _ffn_body(x_ref, wg_ref, wu_ref, wd_ref, g_ref, b_ref, y_ref, yb_ref, xb_ref, acc_ref, *, alpha):
    f = pl.program_id(1)

    @pl.when(f == 0)
    def _():
        xb_ref[...] = x_ref[...].astype(BF16)
        acc_ref[...] = jnp.zeros_like(acc_ref)

    xb = xb_ref[...]
    gate = jnp.dot(xb, wg_ref[...], preferred_element_type=F32)
    up = jnp.dot(xb, wu_ref[...], preferred_element_type=F32)
    h = gate * (1.0 / (1.0 + jnp.exp(-gate))) * up
    acc_ref[...] += jnp.dot(h.astype(BF16), wd_ref[...], preferred_element_type=F32)

    @pl.when(f == pl.num_programs(1) - 1)
    def _():
        y = _layer_norm(alpha * x_ref[...] + acc_ref[...], g_ref[...], b_ref[...])
        y_ref[...] = y
        yb_ref[...] = y.astype(BF16)


def _ffn_ln(x, wg, wu, wd, g, b, alpha, name):
    m, d = x.shape
    dff = wg.shape[1]
    t = _tiles(m)
    tm, tf = t["tm_ln"], t["tf"]
    row = pl.BlockSpec((tm, d), lambda i, f: (i, 0))
    vec = pl.BlockSpec((1, d), lambda i, f: (0, 0))
    return pl.pallas_call(
        functools.partial(_ffn_body, alpha=alpha),
        grid=(m // tm, dff // tf),
        in_specs=[row,
                  pl.BlockSpec((d, tf), lambda i, f: (0, f)),
                  pl.BlockSpec((d, tf), lambda i, f: (0, f)),
                  pl.BlockSpec((tf, d), lambda i, f: (f, 0)),
                  vec, vec],
        out_specs=[row, row],
        out_shape=[jax.ShapeDtypeStruct((m, d), F32), jax.ShapeDtypeStruct((m, d), BF16)],
        scratch_shapes=[pltpu.VMEM((tm, d), BF16), pltpu.VMEM((tm, d), F32)],
        compiler_params=_params(("arbitrary", "arbitrary")),
        name=name,
    )(x, wg, wu, wd, g.reshape(1, d), b.reshape(1, d))


def _toeplitz(vec, rows, cols):
    hh, length = vec.shape
    assert length == rows + cols - 1
    padded = jnp.pad(vec, ((0, 0), (0, 1)))
    skew = jnp.tile(padded, (1, rows))[:, :rows * length].reshape(hh, rows, length)
    return skew[:, :, rows - 1:]


def _split_halves(q):
    lane = lax.broadcasted_iota(jnp.int32, q.shape, 1)
    half = q.shape[1] // 2
    zero = jnp.zeros_like(q)
    return jnp.concatenate([jnp.where(lane < half, q, zero), jnp.where(lane >= half, q, zero)], axis=0)


def _diff_finalize(o, lam_ref, g_ref, lam_init):
    t = o.shape[0] // 2
    lp = lam_ref[...]
    lam = (jnp.exp(jnp.sum(lp[0:1] * lp[1:2], axis=-1, keepdims=True))
           - jnp.exp(jnp.sum(lp[2:3] * lp[3:4], axis=-1, keepdims=True)) + lam_init)
    of = o[:t] - lam * o[t:]
    of = of * lax.rsqrt(jnp.mean(of * of, axis=-1, keepdims=True) + RMS_EPS) * g_ref[...]
    return of * (1.0 - lam_init)


def _neg_abs(z):
    return pltpu.bitcast(pltpu.bitcast(z, jnp.uint32) | jnp.uint32(0x80000000), F32)


def _suffix_matrix(n):
    return jnp.asarray(np.arange(n)[:, None] > np.arange(n)[None, :], BF16)


def _stick_tile(q, kt, vt, u, run, valid):
    z = lax.dot_general(q, kt, _NT, preferred_element_type=F32)
    log_beta = jnp.minimum(z, 0.0) - jnp.log(1.0 + jnp.exp2(_neg_abs(z))) * LOG2E
    log_1m = log_beta - z
    if valid is not None:
        log_1m = jnp.where(valid, log_1m, 0.0)
    suffix = jnp.dot(log_1m.astype(BF16), u, preferred_element_type=F32)
    a = jnp.exp2(log_beta + suffix + run)
    if valid is not None:
        a = jnp.where(valid, a, 0.0)
    return (jnp.dot(a.astype(BF16), vt, preferred_element_type=F32),
            run + jnp.sum(log_1m, axis=-1, keepdims=True))


def _attn_a_body(q_ref, k_ref, v_ref, bias_ref, o_ref, *, tq, nb, heads, hd):
    i = pl.program_id(2)
    starts = [pl.multiple_of(jnp.maximum(i - (nb - 1) + j, 0) * tq, tq) for j in range(nb)]
    for hh in range(heads):
        cols = slice(hh * hd, (hh + 1) * hd)
        q = q_ref[:, cols]
        scores = []
        for j in range(nb):
            s = lax.dot_general(q, k_ref[pl.ds(starts[j], tq), cols], _NT, preferred_element_type=F32)
            s = s + bias_ref[hh, :, j * tq:(j + 1) * tq]
            if j < nb - 1:
                s = jnp.where(i - (nb - 1) + j >= 0, s, NEG_INF)
            scores.append(s)
        m = functools.reduce(jnp.maximum, [jnp.max(s, axis=-1, keepdims=True) for s in scores])
        l = jnp.zeros_like(m)
        acc = jnp.zeros((tq, hd), F32)
        for start, s in zip(starts, scores):
            p = jnp.exp2(s - m)
            l = l + jnp.sum(p, axis=-1, keepdims=True)
            acc = acc + jnp.dot(p.astype(BF16), v_ref[pl.ds(start, tq), cols], preferred_element_type=F32)
        o_ref[:, cols] = (acc / l).astype(BF16)


def _clipped_bias_vec(table, rel):
    idx = np.clip(rel, -REL_CLIP_A, REL_CLIP_A) + REL_CLIP_A
    return jnp.transpose(table[idx]).astype(F32) * LOG2E


def _band_bias_prompt(table, tq, nb):
    w = nb * tq
    r = np.arange(tq)[:, None]
    off = np.arange(w)[None, :] - (nb - 1) * tq
    kc, qc = off // CHUNK, r // CHUNK
    valid = (kc <= qc) & (kc >= qc - BAND_CHUNKS)
    vec = _clipped_bias_vec(table, np.arange(tq + w - 1) - (tq - 1) - (nb - 1) * tq)
    return jnp.where(valid[None], _toeplitz(vec, tq, w), NEG_INF)


def _attn_a_prompt(qkv, table, b, t, h, hd, name):
    cfg = _tiles(b * t, t)
    tq, heads = cfg["tq_band"], cfg["band_heads"]
    assert BAND_PAST % tq == 0 and t % tq == 0 and tq % CHUNK == 0 and h % heads == 0
    nb = BAND_PAST // tq + 1
    nt = t // tq
    hg = h // heads
    bias = _band_bias_prompt(table, tq, nb)
    return pl.pallas_call(
        functools.partial(_attn_a_body, tq=tq, nb=nb, heads=heads, hd=hd),
        grid=(b, hg, nt),
        in_specs=[pl.BlockSpec((tq, heads * hd), lambda bi, gi, i: (bi * nt + i, gi)),
                  pl.BlockSpec((t, heads * hd), lambda bi, gi, i: (bi, hg + gi)),
                  pl.BlockSpec((t, heads * hd), lambda bi, gi, i: (bi, 2 * hg + gi)),
                  pl.BlockSpec((heads, tq, nb * tq), lambda bi, gi, i: (gi, 0, 0))],
        out_specs=pl.BlockSpec((tq, heads * hd), lambda bi, gi, i: (bi * nt + i, gi)),
        out_shape=jax.ShapeDtypeStruct((b * t, h * hd), BF16),
        compiler_params=_params(("arbitrary", "arbitrary", "arbitrary")),
        name=name,
    )(qkv, qkv, qkv, bias)


def _lane_fold(x, op):
    return functools.reduce(op, [x[:, c * LANES:(c + 1) * LANES] for c in range(x.shape[1] // LANES)])


def _attn_b_body(q_ref, k_ref, v_ref, near_ref, far_ref, lam_ref, g_ref, o_ref, acc_ref, mx_ref, l_ref, *,
                 tq, lam_init):
    i = pl.program_id(2)
    qq = _split_halves(q_ref[...])
    far = far_ref[:, 0:1]
    n_far = jnp.maximum(i - 1, 0)

    def far_scores(j):
        start = pl.multiple_of(j * tq, tq)
        return lax.dot_general(qq, k_ref[pl.ds(start, tq), :], _NT, preferred_element_type=F32), start

    def near_scores(j):
        s, start = far_scores(j)
        return (s.reshape(2, tq, tq) + near_ref[j - i + 1][None]).reshape(2 * tq, tq), start

    def max_pass(scores, lo, hi):
        mx_ref[...] = jnp.full_like(mx_ref, NEG_INF)

        def body(j, carry):
            mx_ref[...] = jnp.maximum(mx_ref[...], _lane_fold(scores(j)[0], jnp.maximum))
            return carry

        lax.fori_loop(lo, hi, body, 0)
        return jnp.max(mx_ref[...], axis=-1, keepdims=True)

    m = jnp.maximum(max_pass(far_scores, 0, n_far) + far, max_pass(near_scores, n_far, i + 1))
    mx_ref[...] = jnp.broadcast_to(m, mx_ref.shape)
    l_ref[...] = jnp.zeros_like(l_ref)
    acc_ref[...] = jnp.zeros_like(acc_ref)

    def sum_pass(scores, shift, lo, hi):
        def body(j, carry):
            s, start = scores(j)
            p = jnp.exp2(s - jnp.tile(mx_ref[...] - shift, (1, tq // LANES)))
            l_ref[...] += _lane_fold(p, jnp.add)
            acc_ref[...] += jnp.dot(p.astype(BF16), v_ref[pl.ds(start, tq), :], preferred_element_type=F32)
            return carry

        lax.fori_loop(lo, hi, body, 0)

    sum_pass(far_scores, far, 0, n_far)
    sum_pass(near_scores, 0.0, n_far, i + 1)
    l = jnp.sum(l_ref[...], axis=-1, keepdims=True)
    o_ref[...] = _diff_finalize(acc_ref[...] / l, lam_ref, g_ref, lam_init).astype(BF16)


def _t5_bucket(rel):
    half = T5_BUCKETS // 2
    max_exact = half // 2
    n = jnp.abs(rel)
    nf = jnp.maximum(n, 1).astype(F32)
    large = max_exact + (jnp.log(nf / max_exact) / math.log(T5_MAX_DIST / max_exact)
                         * (half - max_exact)).astype(jnp.int32)
    return jnp.where(rel > 0, half, 0) + jnp.where(n < max_exact, n, jnp.minimum(large, half - 1))


def _t5_far_distance():
    half = T5_BUCKETS // 2
    max_exact = half // 2
    return math.ceil(max_exact * (T5_MAX_DIST / max_exact) ** ((half - 1 - max_exact + 0.5) / (half - max_exact)))


def _t5_bias_vec(table, rel):
    return jnp.transpose(table[_t5_bucket(jnp.asarray(rel, jnp.int32))]).astype(F32) * LOG2E


def _t5_bias_prompt(table, tq):
    assert tq + 1 >= _t5_far_distance()
    r = np.arange(tq)[:, None]
    c = np.arange(tq)[None, :]
    left = _toeplitz(_t5_bias_vec(table, np.arange(2 * tq - 1) - (tq - 1) - tq), tq, tq)
    diag = _toeplitz(_t5_bias_vec(table, np.arange(2 * tq - 1) - (tq - 1)), tq, tq)
    diag = jnp.where(((c // CHUNK) <= (r // CHUNK))[None], diag, NEG_INF)
    far = jnp.broadcast_to(_t5_bias_vec(table, np.array([-(tq + 1)]))[:, :, None], (table.shape[1], 1, LANES))
    return jnp.stack([left, diag], axis=1), far


def _attn_b_prompt(qkv, table, lam_rows, gain, lam_init, b, t, h, hd, name):
    tq = _tiles(b * t, t)["tq_diff"]
    assert t % tq == 0 and tq % CHUNK == 0
    nt = t // tq
    near, far = _t5_bias_prompt(table, tq)
    return pl.pallas_call(
        functools.partial(_attn_b_body, tq=tq, lam_init=lam_init),
        grid=(b, h, nt),
        in_specs=[pl.BlockSpec((tq, hd), lambda bi, hi, i: (bi * nt + i, hi)),
                  pl.BlockSpec((t, hd), lambda bi, hi, i: (bi, h + hi)),
                  pl.BlockSpec((t, hd), lambda bi, hi, i: (bi, 2 * h + hi)),
                  pl.BlockSpec((None, 2, tq, tq), lambda bi, hi, i: (hi, 0, 0, 0)),
                  pl.BlockSpec((None, 1, LANES), lambda bi, hi, i: (hi, 0, 0)),
                  pl.BlockSpec(lam_rows.shape, lambda bi, hi, i: (0, 0)),
                  pl.BlockSpec((1, hd), lambda bi, hi, i: (0, 0))],
        out_specs=pl.BlockSpec((tq, hd), lambda bi, hi, i: (bi * nt + i, hi)),
        out_shape=jax.ShapeDtypeStruct((b * t, h * hd), BF16),
        scratch_shapes=[pltpu.VMEM((2 * tq, hd), F32), pltpu.VMEM((2 * tq, LANES), F32),
                        pltpu.VMEM((2 * tq, LANES), F32)],
        compiler_params=_params(("arbitrary", "arbitrary", "arbitrary")),
        name=name,
    )(qkv, qkv, qkv, near, far, lam_rows, gain.reshape(1, hd))


def _attn_c_body(q_ref, k_ref, v_ref, u_ref, o_ref, acc_ref, *, tq, tk):
    i = pl.program_id(2)
    q = q_ref[...]
    u = u_ref[...]
    subs = tq // tk
    row = lax.broadcasted_iota(jnp.int32, (tq, tk), 0)
    col = lax.broadcasted_iota(jnp.int32, (tq, tk), 1)

    def group(base, run, masked):
        total = None
        for sub in reversed(range(subs)):
            start = pl.multiple_of(base + sub * tk, tk)
            valid = (col + sub * tk < row) if masked else None
            pv, run = _stick_tile(q, k_ref[pl.ds(start, tk), :], v_ref[pl.ds(start, tk), :], u, run, valid)
            total = pv if total is None else total + pv
        return total, run

    pv, run = group(i * tq, jnp.zeros((tq, 1), F32), True)
    acc_ref[...] = pv

    def body(step, run):
        pv, run = group((i - 1 - step) * tq, run, False)
        acc_ref[...] += pv
        return run

    lax.fori_loop(0, i, body, run)
    o_ref[...] = acc_ref[...].astype(BF16)


def _attn_c_prompt(qkv, b, t, h, hd, name):
    cfg = _tiles(b * t, t)
    tq, tk = cfg["tq_stick"], min(cfg["tk_stick"], cfg["tq_stick"])
    assert t % tq == 0 and tq % tk == 0
    nt = t // tq
    return pl.pallas_call(
        functools.partial(_attn_c_body, tq=tq, tk=tk),
        grid=(b, h, nt),
        in_specs=[pl.BlockSpec((tq, hd), lambda bi, hi, i: (bi * nt + i, hi)),
                  pl.BlockSpec((t, hd), lambda bi, hi, i: (bi, h + hi)),
                  pl.BlockSpec((t, hd), lambda bi, hi, i: (bi, 2 * h + hi)),
                  pl.BlockSpec((tk, tk), lambda bi, hi, i: (0, 0))],
        out_specs=pl.BlockSpec((tq, hd), lambda bi, hi, i: (bi * nt + i, hi)),
        out_shape=jax.ShapeDtypeStruct((b * t, h * hd), BF16),
        scratch_shapes=[pltpu.VMEM((tq, hd), F32)],
        compiler_params=_params(("arbitrary", "arbitrary", "arbitrary")),
        name=name,
    )(qkv, qkv, qkv, _suffix_matrix(tk))


def _samp_softmax_body(q_ref, kn_ref, vn_ref, kc_ref, vc_ref, bc_ref, bn_ref, *rest, hd, diff, lam_init):
    o_ref = rest[-1]
    t = q_ref.shape[0]
    reps = 2 if diff else 1
    for hh in range(SUBLANES):
        cols = slice(hh * hd, (hh + 1) * hd)
        q = q_ref[:, cols]
        qq = _split_halves(q) if diff else q

        def scores(k, bias):
            s = lax.dot_general(qq, k, _NT, preferred_element_type=F32)
            return (s.reshape(reps, t, -1) + bias[None]).reshape(reps * t, -1)

        s_c = scores(kc_ref[:, hh, :].astype(BF16), bc_ref[hh])
        s_n = scores(kn_ref[:, cols], bn_ref[hh])
        m = jnp.maximum(jnp.max(s_c, axis=-1, keepdims=True), jnp.max(s_n, axis=-1, keepdims=True))
        p_c = jnp.exp2(s_c - m)
        p_n = jnp.exp2(s_n - m)
        l = jnp.sum(p_c, axis=-1, keepdims=True) + jnp.sum(p_n, axis=-1, keepdims=True)
        acc = (jnp.dot(p_c.astype(BF16), vc_ref[:, hh, :].astype(BF16), preferred_element_type=F32)
               + jnp.dot(p_n.astype(BF16), vn_ref[:, cols], preferred_element_type=F32))
        o = acc / l
        if diff:
            o = _diff_finalize(o, rest[0], rest[1], lam_init)
        o_ref[:, cols] = o.astype(BF16)


def _sample_specs(cache_k, cache_v, layer, bs, t, h, hd):
    assert h % SUBLANES == 0
    groups = h // SUBLANES
    width = SUBLANES * hd
    r = cache_k.shape[2]
    ck = cache_k.reshape(cache_k.shape[0], bs, r, groups, SUBLANES, hd)
    cv = cache_v.reshape(cache_v.shape[0], bs, r, groups, SUBLANES, hd)
    cache_spec = pl.BlockSpec((None, None, r, None, SUBLANES, hd), lambda bi, gi: (layer, bi, 0, gi, 0, 0))
    specs = [pl.BlockSpec((t, width), lambda bi, gi: (bi, gi)),
             pl.BlockSpec((t, width), lambda bi, gi: (bi, groups + gi)),
             pl.BlockSpec((t, width), lambda bi, gi: (bi, 2 * groups + gi)),
             cache_spec, cache_spec]
    out_spec = pl.BlockSpec((t, width), lambda bi, gi: (bi, gi))
    return (bs, groups), specs, out_spec, ck, cv


def _attn_softmax_sample(qkv, cache_k, cache_v, layer, bias, bs, t, h, hd, name, diff_args=None):
    r = cache_k.shape[2]
    grid, in_specs, out_spec, ck, cv = _sample_specs(cache_k, cache_v, layer, bs, t, h, hd)
    in_specs += [pl.BlockSpec((SUBLANES, t, r), lambda bi, gi: (gi, 0, 0)),
                 pl.BlockSpec((SUBLANES, t, t), lambda bi, gi: (gi, 0, 0))]
    args = [qkv, qkv, qkv, ck, cv, bias[:, :, :r], bias[:, :, r:]]
    lam_init = None
    if diff_args is not None:
        lam_rows, gain, lam_init = diff_args
        in_specs += [pl.BlockSpec(lam_rows.shape, lambda bi, gi: (0, 0)),
                     pl.BlockSpec((1, hd), lambda bi, gi: (0, 0))]
        args += [lam_rows, gain.reshape(1, hd)]
    return pl.pallas_call(
        functools.partial(_samp_softmax_body, hd=hd, diff=diff_args is not None, lam_init=lam_init),
        grid=grid,
        in_specs=in_specs,
        out_specs=out_spec,
        out_shape=jax.ShapeDtypeStruct((bs * t, h * hd), BF16),
        compiler_params=_params(("arbitrary", "arbitrary")),
        name=name,
    )(*args)


def _samp_stick_body(q_ref, kn_ref, vn_ref, kc_ref, vc_ref, un_ref, u_ref, o_ref, *, tk, hd):
    t = q_ref.shape[0]
    valid = lax.broadcasted_iota(jnp.int32, (t, t), 1) < lax.broadcasted_iota(jnp.int32, (t, t), 0)
    for hh in range(SUBLANES):
        cols = slice(hh * hd, (hh + 1) * hd)
        q = q_ref[:, cols]
        acc, run = _stick_tile(q, kn_ref[:, cols], vn_ref[:, cols], un_ref[...], jnp.zeros((t, 1), F32), valid)
        for tile in reversed(range(kc_ref.shape[0] // tk)):
            rows = slice(tile * tk, (tile + 1) * tk)
            pv, run = _stick_tile(q, kc_ref[rows, hh, :].astype(BF16), vc_ref[rows, hh, :].astype(BF16),
                                  u_ref[...], run, None)
            acc = acc + pv
        o_ref[:, cols] = acc.astype(BF16)


def _attn_stick_sample(qkv, cache_k, cache_v, layer, bs, t, h, hd, name):
    r = cache_k.shape[2]
    tk = MXU_DIM if r % MXU_DIM == 0 else LANES
    assert r % tk == 0
    grid, in_specs, out_spec, ck, cv = _sample_specs(cache_k, cache_v, layer, bs, t, h, hd)
    in_specs += [pl.BlockSpec((t, t), lambda bi, gi: (0, 0)),
                 pl.BlockSpec((tk, tk), lambda bi, gi: (0, 0))]
    return pl.pallas_call(
        functools.partial(_samp_stick_body, tk=tk, hd=hd),
        grid=grid,
        in_specs=in_specs,
        out_specs=out_spec,
        out_shape=jax.ShapeDtypeStruct((bs * t, h * hd), BF16),
        compiler_params=_params(("arbitrary", "arbitrary")),
        name=name,
    )(qkv, qkv, qkv, ck, cv, _suffix_matrix(t), _suffix_matrix(tk))


def _band_bias_sample(table, past_len, r, t):
    qpos = past_len + np.arange(t)
    kpos = past_len - r + np.arange(r + t)
    qc, kc = qpos[:, None] // CHUNK, kpos[None, :] // CHUNK
    valid = (kc <= qc) & (kc >= qc - BAND_CHUNKS)
    vec = _clipped_bias_vec(table, np.arange(r + 2 * t - 1) - (t - 1) + (kpos[0] - qpos[0]))
    return jnp.where(valid[None], _toeplitz(vec, t, r + t), NEG_INF)


def _t5_bias_sample(table, r, t):
    qpos = r + np.arange(t)
    kpos = np.arange(r + t)
    valid = (kpos[None, :] // CHUNK) <= (qpos[:, None] // CHUNK)
    vec = _t5_bias_vec(table, np.arange(r + 2 * t - 1) - (t - 1) + (kpos[0] - qpos[0]))
    return jnp.where(valid[None], _toeplitz(vec, t, r + t), NEG_INF)


def kernel(x_prompt, x_sample, cache_a_k, cache_a_v, cache_b_k, cache_b_v, cache_c_k, cache_c_v, w_in_a, w_out_a, rel_bias_a, w_in_b, w_out_b, lambda_q1, lambda_k1, lambda_q2, lambda_k2, diff_norm_g, t5_bias, w_in_c, w_out_c, ln1_g, ln1_b, ln2_g, ln2_b, w_gate, w_up, w_down):
    b, t, d = x_prompt.shape
    bs, ts, _ = x_sample.shape
    h, hd = cache_a_k.shape[3], cache_a_k.shape[4]
    depth = ln1_g.shape[0]
    alpha = (2 * depth) ** 0.25
    past_len = cache_b_k.shape[2]

    yp = x_prompt.reshape(b * t, d)
    ys = x_sample.reshape(bs * ts, d)
    yp_b, ys_b = yp.astype(BF16), ys.astype(BF16)
    new = {key: [] for key in ("pa_k", "pa_v", "pb_k", "pb_v", "pc_k", "pc_v",
                               "sa_k", "sa_v", "sb_k", "sb_v", "sc_k", "sc_v")}
    for i in range(depth):
        kind, j = i % N_MIXERS, i // N_MIXERS
        w_in = (w_in_a, w_in_b, w_in_c)[kind][j].astype(BF16)
        w_out = (w_out_a, w_out_b, w_out_c)[kind][j].astype(BF16)
        q_scale = ((hd // 2) ** -0.5 if kind == 1 else hd ** -0.5) * LOG2E
        qkv_p, kp, vp = _qkv_proj(yp_b, w_in, q_scale, hd, kind != 0, f"qkv_prompt_{i}")
        qkv_s, ks, vs = _qkv_proj(ys_b, w_in, q_scale, hd, True, f"qkv_sample_{i}")
        kp, vp = kp.reshape(b, t, h, hd), vp.reshape(b, t, h, hd)
        ks, vs = ks.reshape(bs, ts, h, hd), vs.reshape(bs, ts, h, hd)
        if kind == 0:
            op = _attn_a_prompt(qkv_p, rel_bias_a[j], b, t, h, hd, f"band_prompt_{i}")
            r = cache_a_k.shape[2]
            os_ = _attn_softmax_sample(qkv_s, cache_a_k, cache_a_v, j,
                                       _band_bias_sample(rel_bias_a[j], past_len, r, ts),
                                       bs, ts, h, hd, f"band_sample_{i}")
            keep = min(BAND_PAST, t)
            kp, vp = kp[:, -keep:], vp[:, -keep:]
            tag = "a"
        elif kind == 1:
            lam_init = 0.8 - 0.6 * math.exp(-0.3 * i)
            lam_rows = jnp.stack([lambda_q1[j], lambda_k1[j], lambda_q2[j], lambda_k2[j]]).astype(F32)
            op = _attn_b_prompt(qkv_p, t5_bias, lam_rows, diff_norm_g[j], lam_init, b, t, h, hd,
                                f"diff_prompt_{i}")
            r = cache_b_k.shape[2]
            os_ = _attn_softmax_sample(qkv_s, cache_b_k, cache_b_v, j, _t5_bias_sample(t5_bias, r, ts),
                                       bs, ts, h, hd, f"diff_sample_{i}",
                                       diff_args=(lam_rows, diff_norm_g[j], lam_init))
            tag = "b"
        else:
            op = _attn_c_prompt(qkv_p, b, t, h, hd, f"stick_prompt_{i}")
            os_ = _attn_stick_sample(qkv_s, cache_c_k, cache_c_v, j, bs, ts, h, hd, f"stick_sample_{i}")
            tag = "c"
        new[f"p{tag}_k"].append(kp)
        new[f"p{tag}_v"].append(vp)
        new[f"s{tag}_k"].append(ks)
        new[f"s{tag}_v"].append(vs)
        yp = _outproj_ln(op, yp, w_out, ln1_g[i], ln1_b[i], alpha, f"outproj_prompt_{i}")
        ys = _outproj_ln(os_, ys, w_out, ln1_g[i], ln1_b[i], alpha, f"outproj_sample_{i}")
        wg, wu, wd = w_gate[i].astype(BF16), w_up[i].astype(BF16), w_down[i].astype(BF16)
        yp, yp_b = _ffn_ln(yp, wg, wu, wd, ln2_g[i], ln2_b[i], alpha, f"ffn_prompt_{i}")
        ys, ys_b = _ffn_ln(ys, wg, wu, wd, ln2_g[i], ln2_b[i], alpha, f"ffn_sample_{i}")
    return (yp.reshape(b, t, d), ys.reshape(bs, ts, d),
            jnp.stack(new["pa_k"]), jnp.stack(new["pa_v"]), jnp.stack(new["pb_k"]), jnp.stack(new["pb_v"]),
            jnp.stack(new["pc_k"]), jnp.stack(new["pc_v"]), jnp.stack(new["sa_k"]), jnp.stack(new["sa_v"]),
            jnp.stack(new["sb_k"]), jnp.stack(new["sb_v"]), jnp.stack(new["sc_k"]), jnp.stack(new["sc_v"]))
```

```python
import functools
import math

import jax
import jax.numpy as jnp
import numpy as np
from jax import lax
from jax.experimental import pallas as pl
from jax.experimental.pallas import tpu as pltpu

F32 = jnp.float32
BF16 = jnp.bfloat16

CHUNK = 64
N_MIXERS = 3
BAND_CHUNKS = 8
BAND_PAST = BAND_CHUNKS * CHUNK
REL_CLIP_A = 256
T5_BUCKETS = 32
T5_MAX_DIST = 128
LN_EPS = 1e-5
RMS_EPS = 1e-5
NEG_INF = -1e30
LOG2E = math.log2(math.e)

LANES = 128
SUBLANES = 8
MXU_DIM = 256
V7X_VMEM_LIMIT_BYTES = 56 * 1024 * 1024

_NT = (((1,), (1,)), ((), ()))


def _params(semantics):
    return pltpu.CompilerParams(dimension_semantics=semantics, vmem_limit_bytes=V7X_VMEM_LIMIT_BYTES)


def _tiles(m, t=None):
    tm = next(c for c in (1024, 512, m) if m % c == 0)
    cfg = dict(tm=tm, tm_ln=min(tm, 512), tf=512)
    if t is not None:
        cfg.update(tq_band=min(256, t), band_heads=4, tq_diff=min(512, t), tq_stick=min(512, t),
                   tk_stick=MXU_DIM)
    return cfg


def _layer_norm(y, g, b):
    yc = y - jnp.mean(y, axis=-1, keepdims=True)
    var = jnp.mean(yc * yc, axis=-1, keepdims=True)
    return yc * lax.rsqrt(var + LN_EPS) * g + b


def _qkv_body(x_ref, w_ref, qkv_ref, k32_ref, v32_ref, *, nq, hd, q_scale, head_major):
    j = pl.program_id(1)

    def proj():
        return jnp.dot(x_ref[...], w_ref[...], preferred_element_type=F32)

    def store_f32(ref, acc):
        if head_major:
            for hh in range(SUBLANES):
                ref[:, hh, :] = acc[:, hh * hd:(hh + 1) * hd]
        else:
            ref[...] = acc

    @pl.when(j < nq)
    def _():
        qkv_ref[...] = (proj() * q_scale).astype(BF16)

    @pl.when((j >= nq) & (j < 2 * nq))
    def _():
        acc = proj()
        qkv_ref[...] = acc.astype(BF16)
        store_f32(k32_ref, acc)

    @pl.when(j >= 2 * nq)
    def _():
        acc = proj()
        qkv_ref[...] = acc.astype(BF16)
        store_f32(v32_ref, acc)


def _qkv_proj(x, w, q_scale, hd, head_major, name):
    m, d = x.shape
    tm = _tiles(m)["tm"]
    tn = SUBLANES * hd
    nq = d // tn

    def f32_spec(first):
        if head_major:
            return pl.BlockSpec((tm, None, SUBLANES, hd), lambda i, j: (i, jnp.clip(j - first, 0, nq - 1), 0, 0))
        return pl.BlockSpec((tm, tn), lambda i, j: (i, jnp.clip(j - first, 0, nq - 1)))

    f32_shape = jax.ShapeDtypeStruct((m, nq, SUBLANES, hd) if head_major else (m, d), F32)
    qkv, k32, v32 = pl.pallas_call(
        functools.partial(_qkv_body, nq=nq, hd=hd, q_scale=q_scale, head_major=head_major),
        grid=(m // tm, 3 * nq),
        in_specs=[pl.BlockSpec((tm, d), lambda i, j: (i, 0)),
                  pl.BlockSpec((d, tn), lambda i, j: (0, j))],
        out_specs=[pl.BlockSpec((tm, tn), lambda i, j: (i, j)), f32_spec(nq), f32_spec(2 * nq)],
        out_shape=[jax.ShapeDtypeStruct((m, 3 * d), BF16), f32_shape, f32_shape],
        compiler_params=_params(("arbitrary", "arbitrary")),
        name=name,
    )(x, w)
    return qkv, k32.reshape(m, nq * SUBLANES, hd), v32.reshape(m, nq * SUBLANES, hd)


def _outproj_body(o_ref, x_ref, w_ref, g_ref, b_ref, y_ref, *, alpha):
    y = alpha * x_ref[...] + jnp.dot(o_ref[...], w_ref[...], preferred_element_type=F32)
    y_ref[...] = _layer_norm(y, g_ref[...], b_ref[...])


def _outproj_ln(o, x, w, g, b, alpha, name):
    m, d = x.shape
    tm = _tiles(m)["tm_ln"]
    row = pl.BlockSpec((tm, d), lambda i: (i, 0))
    vec = pl.BlockSpec((1, d), lambda i: (0, 0))
    return pl.pallas_call(
        functools.partial(_outproj_body, alpha=alpha),
        grid=(m // tm,),
        in_specs=[row, row, pl.BlockSpec((d, d), lambda i: (0, 0)), vec, vec],
        out_specs=row,
        out_shape=jax.ShapeDtypeStruct((m, d), F32),
        compiler_params=_params(("arbitrary",)),
        name=name,
    )(o, x, w, g.reshape(1, d), b.reshape(1, d))


def _ffn_body(x_ref, wg_ref, wu_ref, wd_ref, g_ref, b_ref, y_ref, yb_ref, xb_ref, acc_ref, *, alpha):
    f = pl.program_id(1)

    @pl.when(f == 0)
    def _():
        xb_ref[...] = x_ref[...].astype(BF16)
        acc_ref[...] = jnp.zeros_like(acc_ref)

    xb = xb_ref[...]
    gate = jnp.dot(xb, wg_ref[...], preferred_element_type=F32)
    up = jnp.dot(xb, wu_ref[...], preferred_element_type=F32)
    h = gate * (1.0 / (1.0 + jnp.exp(-gate))) * up
    acc_ref[...] += jnp.dot(h.astype(BF16), wd_ref[...], preferred_element_type=F32)

    @pl.when(f == pl.num_programs(1) - 1)
    def _():
        y = _layer_norm(alpha * x_ref[...] + acc_ref[...], g_ref[...], b_ref[...])
        y_ref[...] = y
        yb_ref[...] = y.astype(BF16)


def _ffn_ln(x, wg, wu, wd, g, b, alpha, name):
    m, d = x.shape
    dff = wg.shape[1]
    t = _tiles(m)
    tm, tf = t["tm_ln"], t["tf"]
    row = pl.BlockSpec((tm, d), lambda i, f: (i, 0))
    vec = pl.BlockSpec((1, d), lambda i, f: (0, 0))
    return pl.pallas_call(
        functools.partial(_ffn_body, alpha=alpha),
        grid=(m // tm, dff // tf),
        in_specs=[row,
                  pl.BlockSpec((d, tf), lambda i, f: (0, f)),
                  pl.BlockSpec((d, tf), lambda i, f: (0, f)),
                  pl.BlockSpec((tf, d), lambda i, f: (f, 0)),
                  vec, vec],
        out_specs=[row, row],
        out_shape=[jax.ShapeDtypeStruct((m, d), F32), jax.ShapeDtypeStruct((m, d), BF16)],
        scratch_shapes=[pltpu.VMEM((tm, d), BF16), pltpu.VMEM((tm, d), F32)],
        compiler_params=_params(("arbitrary", "arbitrary")),
        name=name,
    )(x, wg, wu, wd, g.reshape(1, d), b.reshape(1, d))


def _toeplitz(vec, rows, cols):
    hh, length = vec.shape
    assert length == rows + cols - 1
    padded = jnp.pad(vec, ((0, 0), (0, 1)))
    skew = jnp.tile(padded, (1, rows))[:, :rows * length].reshape(hh, rows, length)
    return skew[:, :, rows - 1:]


def _split_halves(q):
    lane = lax.broadcasted_iota(jnp.int32, q.shape, 1)
    half = q.shape[1] // 2
    zero = jnp.zeros_like(q)
    return jnp.concatenate([jnp.where(lane < half, q, zero), jnp.where(lane >= half, q, zero)], axis=0)


def _diff_finalize(o, lam_ref, g_ref, lam_init):
    t = o.shape[0] // 2
    lp = lam_ref[...]
    lam = (jnp.exp(jnp.sum(lp[0:1] * lp[1:2], axis=-1, keepdims=True))
           - jnp.exp(jnp.sum(lp[2:3] * lp[3:4], axis=-1, keepdims=True)) + lam_init)
    of = o[:t] - lam * o[t:]
    of = of * lax.rsqrt(jnp.mean(of * of, axis=-1, keepdims=True) + RMS_EPS) * g_ref[...]
    return of * (1.0 - lam_init)


def _neg_abs(z):
    return pltpu.bitcast(pltpu.bitcast(z, jnp.uint32) | jnp.uint32(0x80000000), F32)


def _suffix_matrix(n):
    return jnp.asarray(np.arange(n)[:, None] > np.arange(n)[None, :], BF16)


def _stick_tile(q, kt, vt, u, run, valid):
    z = lax.dot_general(q, kt, _NT, preferred_element_type=F32)
    log_beta = jnp.minimum(z, 0.0) - jnp.log(1.0 + jnp.exp2(_neg_abs(z))) * LOG2E
    log_1m = log_beta - z
    if valid is not None:
        log_1m = jnp.where(valid, log_1m, 0.0)
    suffix = jnp.dot(log_1m.astype(BF16), u, preferred_element_type=F32)
    a = jnp.exp2(log_beta + suffix + run)
    if valid is not None:
        a = jnp.where(valid, a, 0.0)
    return (jnp.dot(a.astype(BF16), vt, preferred_element_type=F32),
            run + jnp.sum(log_1m, axis=-1, keepdims=True))


def _attn_a_body(q_ref, k_ref, v_ref, bias_ref, o_ref, *, tq, nb, heads, hd):
    i = pl.program_id(2)
    starts = [pl.multiple_of(jnp.maximum(i - (nb - 1) + j, 0) * tq, tq) for j in range(nb)]
    for hh in range(heads):
        cols = slice(hh * hd, (hh + 1) * hd)
        q = q_ref[:, cols]
        scores = []
        for j in range(nb):
            s = lax.dot_general(q, k_ref[pl.ds(starts[j], tq), cols], _NT, preferred_element_type=F32)
            s = s + bias_ref[hh, :, j * tq:(j + 1) * tq]
            if j < nb - 1:
                s = jnp.where(i - (nb - 1) + j >= 0, s, NEG_INF)
            scores.append(s)
        m = functools.reduce(jnp.maximum, [jnp.max(s, axis=-1, keepdims=True) for s in scores])
        l = jnp.zeros_like(m)
        acc = jnp.zeros((tq, hd), F32)
        for start, s in zip(starts, scores):
            p = jnp.exp2(s - m)
            l = l + jnp.sum(p, axis=-1, keepdims=True)
            acc = acc + jnp.dot(p.astype(BF16), v_ref[pl.ds(start, tq), cols], preferred_element_type=F32)
        o_ref[:, cols] = (acc / l).astype(BF16)


def _clipped_bias_vec(table, rel):
    idx = np.clip(rel, -REL_CLIP_A, REL_CLIP_A) + REL_CLIP_A
    return jnp.transpose(table[idx]).astype(F32) * LOG2E


def _band_bias_prompt(table, tq, nb):
    w = nb * tq
    r = np.arange(tq)[:, None]
    off = np.arange(w)[None, :] - (nb - 1) * tq
    kc, qc = off // CHUNK, r // CHUNK
    valid = (kc <= qc) & (kc >= qc - BAND_CHUNKS)
    vec = _clipped_bias_vec(table, np.arange(tq + w - 1) - (tq - 1) - (nb - 1) * tq)
    return jnp.where(valid[None], _toeplitz(vec, tq, w), NEG_INF)


def _attn_a_prompt(qkv, table, b, t, h, hd, name):
    cfg = _tiles(b * t, t)
    tq, heads = cfg["tq_band"], cfg["band_heads"]
    assert BAND_PAST % tq == 0 and t % tq == 0 and tq % CHUNK == 0 and h % heads == 0
    nb = BAND_PAST // tq + 1
    nt = t // tq
    hg = h // heads
    bias = _band_bias_prompt(table, tq, nb)
    return pl.pallas_call(
        functools.partial(_attn_a_body, tq=tq, nb=nb, heads=heads, hd=hd),
        grid=(b, hg, nt),
        in_specs=[pl.BlockSpec((tq, heads * hd), lambda bi, gi, i: (bi * nt + i, gi)),
                  pl.BlockSpec((t, heads * hd), lambda bi, gi, i: (bi, hg + gi)),
                  pl.BlockSpec((t, heads * hd), lambda bi, gi, i: (bi, 2 * hg + gi)),
                  pl.BlockSpec((heads, tq, nb * tq), lambda bi, gi, i: (gi, 0, 0))],
        out_specs=pl.BlockSpec((tq, heads * hd), lambda bi, gi, i: (bi * nt + i, gi)),
        out_shape=jax.ShapeDtypeStruct((b * t, h * hd), BF16),
        compiler_params=_params(("arbitrary", "arbitrary", "arbitrary")),
        name=name,
    )(qkv, qkv, qkv, bias)


def _lane_fold(x, op):
    return functools.reduce(op, [x[:, c * LANES:(c + 1) * LANES] for c in range(x.shape[1] // LANES)])


STABILISER_SLACK = 64.0


def _attn_b_body(q_ref, k_ref, v_ref, near_ref, far_ref, lam_ref, g_ref, o_ref, acc_ref, stab_ref, l_ref,
                 track_ref, *, tq, lam_init):
    i = pl.program_id(2)
    qq = _split_halves(q_ref[...])
    far = far_ref[:, 0:1]
    n_far = jnp.maximum(i - 1, 0)

    def far_scores(j):
        start = pl.multiple_of(j * tq, tq)
        return lax.dot_general(qq, k_ref[pl.ds(start, tq), :], _NT, preferred_element_type=F32), start

    def near_scores(j):
        s, start = far_scores(j)
        return (s.reshape(2, tq, tq) + near_ref[j - i + 1][None]).reshape(2 * tq, tq), start

    def near_max():
        track_ref[...] = jnp.full_like(track_ref, NEG_INF)

        def body(j, carry):
            track_ref[...] = jnp.maximum(track_ref[...], _lane_fold(near_scores(j)[0], jnp.maximum))
            return carry

        lax.fori_loop(n_far, i + 1, body, 0)
        return jnp.max(track_ref[...], axis=-1, keepdims=True)

    def sum_pass(scores, shift, lo, hi, track):
        def body(j, carry):
            s, start = scores(j)
            if track:
                track_ref[...] = jnp.maximum(track_ref[...], _lane_fold(s, jnp.maximum))
            p = jnp.exp2(s - jnp.tile(stab_ref[...] - shift, (1, tq // LANES)))
            l_ref[...] += _lane_fold(p, jnp.add)
            acc_ref[...] += jnp.dot(p.astype(BF16), v_ref[pl.ds(start, tq), :], preferred_element_type=F32)
            return carry

        lax.fori_loop(lo, hi, body, 0)

    def sum_all(m, track):
        stab_ref[...] = jnp.broadcast_to(m, stab_ref.shape)
        l_ref[...] = jnp.zeros_like(l_ref)
        acc_ref[...] = jnp.zeros_like(acc_ref)
        sum_pass(far_scores, far, 0, n_far, track)
        sum_pass(near_scores, 0.0, n_far, i + 1, False)

    def finish():
        l = jnp.sum(l_ref[...], axis=-1, keepdims=True)
        o_ref[...] = _diff_finalize(acc_ref[...] / l, lam_ref, g_ref, lam_init).astype(BF16)

    m_near = near_max()
    track_ref[...] = jnp.full_like(track_ref, NEG_INF)
    sum_all(m_near, True)
    m_far = jnp.max(track_ref[...], axis=-1, keepdims=True) + far
    within_slack = jnp.max(m_far - m_near) <= STABILISER_SLACK

    @pl.when(within_slack)
    def _():
        finish()

    @pl.when(jnp.logical_not(within_slack))
    def _():
        sum_all(jnp.maximum(m_far, m_near), False)
        finish()


def _t5_bucket(rel):
    half = T5_BUCKETS // 2
    max_exact = half // 2
    n = jnp.abs(rel)
    nf = jnp.maximum(n, 1).astype(F32)
    large = max_exact + (jnp.log(nf / max_exact) / math.log(T5_MAX_DIST / max_exact)
                         * (half - max_exact)).astype(jnp.int32)
    return jnp.where(rel > 0, half, 0) + jnp.where(n < max_exact, n, jnp.minimum(large, half - 1))


def _t5_far_distance():
    half = T5_BUCKETS // 2
    max_exact = half // 2
    return math.ceil(max_exact * (T5_MAX_DIST / max_exact) ** ((half - 1 - max_exact + 0.5) / (half - max_exact)))


def _t5_bias_vec(table, rel):
    return jnp.transpose(table[_t5_bucket(jnp.asarray(rel, jnp.int32))]).astype(F32) * LOG2E


def _t5_bias_prompt(table, tq):
    assert tq + 1 >= _t5_far_distance()
    r = np.arange(tq)[:, None]
    c = np.arange(tq)[None, :]
    left = _toeplitz(_t5_bias_vec(table, np.arange(2 * tq - 1) - (tq - 1) - tq), tq, tq)
    diag = _toeplitz(_t5_bias_vec(table, np.arange(2 * tq - 1) - (tq - 1)), tq, tq)
    diag = jnp.where(((c // CHUNK) <= (r // CHUNK))[None], diag, NEG_INF)
    far = jnp.broadcast_to(_t5_bias_vec(table, np.array([-(tq + 1)]))[:, :, None], (table.shape[1], 1, LANES))
    return jnp.stack([left, diag], axis=1), far


def _attn_b_prompt(qkv, table, lam_rows, gain, lam_init, b, t, h, hd, name):
    tq = _tiles(b * t, t)["tq_diff"]
    assert t % tq == 0 and tq % CHUNK == 0
    nt = t // tq
    near, far = _t5_bias_prompt(table, tq)
    return pl.pallas_call(
        functools.partial(_attn_b_body, tq=tq, lam_init=lam_init),
        grid=(b, h, nt),
        in_specs=[pl.BlockSpec((tq, hd), lambda bi, hi, i: (bi * nt + i, hi)),
                  pl.BlockSpec((t, hd), lambda bi, hi, i: (bi, h + hi)),
                  pl.BlockSpec((t, hd), lambda bi, hi, i: (bi, 2 * h + hi)),
                  pl.BlockSpec((None, 2, tq, tq), lambda bi, hi, i: (hi, 0, 0, 0)),
                  pl.BlockSpec((None, 1, LANES), lambda bi, hi, i: (hi, 0, 0)),
                  pl.BlockSpec(lam_rows.shape, lambda bi, hi, i: (0, 0)),
                  pl.BlockSpec((1, hd), lambda bi, hi, i: (0, 0))],
        out_specs=pl.BlockSpec((tq, hd), lambda bi, hi, i: (bi * nt + i, hi)),
        out_shape=jax.ShapeDtypeStruct((b * t, h * hd), BF16),
        scratch_shapes=[pltpu.VMEM((2 * tq, hd), F32)] + [pltpu.VMEM((2 * tq, LANES), F32)] * 3,
        compiler_params=_params(("arbitrary", "arbitrary", "arbitrary")),
        name=name,
    )(qkv, qkv, qkv, near, far, lam_rows, gain.reshape(1, hd))


def _attn_c_body(q_ref, k_ref, v_ref, u_ref, o_ref, acc_ref, *, tq, tk):
    i = pl.program_id(2)
    q = q_ref[...]
    u = u_ref[...]
    subs = tq // tk
    below_diag = (lax.broadcasted_iota(jnp.int32, (tk, tk), 1) < lax.broadcasted_iota(jnp.int32, (tk, tk), 0))

    def group(base, run):
        total = None
        for sub in reversed(range(subs)):
            start = pl.multiple_of(base + sub * tk, tk)
            pv, run = _stick_tile(q, k_ref[pl.ds(start, tk), :], v_ref[pl.ds(start, tk), :], u, run, None)
            total = pv if total is None else total + pv
        return total, run

    runs = []
    for rb in range(subs):
        rows = slice(rb * tk, (rb + 1) * tk)
        total, run = None, jnp.zeros((tk, 1), F32)
        for sub in reversed(range(rb + 1)):
            start = pl.multiple_of(i * tq + sub * tk, tk)
            pv, run = _stick_tile(q[rows], k_ref[pl.ds(start, tk), :], v_ref[pl.ds(start, tk), :], u, run,
                                  below_diag if sub == rb else None)
            total = pv if total is None else total + pv
        acc_ref[rows, :] = total
        runs.append(run)
    run = jnp.concatenate(runs, axis=0)

    def body(step, run):
        pv, run = group((i - 1 - step) * tq, run)
        acc_ref[...] += pv
        return run

    lax.fori_loop(0, i, body, run)
    o_ref[...] = acc_ref[...].astype(BF16)


def _attn_c_prompt(qkv, b, t, h, hd, name):
    cfg = _tiles(b * t, t)
    tq, tk = cfg["tq_stick"], min(cfg["tk_stick"], cfg["tq_stick"])
    assert t % tq == 0 and tq % tk == 0
    nt = t // tq
    return pl.pallas_call(
        functools.partial(_attn_c_body, tq=tq, tk=tk),
        grid=(b, h, nt),
        in_specs=[pl.BlockSpec((tq, hd), lambda bi, hi, i: (bi * nt + i, hi)),
                  pl.BlockSpec((t, hd), lambda bi, hi, i: (bi, h + hi)),
                  pl.BlockSpec((t, hd), lambda bi, hi, i: (bi, 2 * h + hi)),
                  pl.BlockSpec((tk, tk), lambda bi, hi, i: (0, 0))],
        out_specs=pl.BlockSpec((tq, hd), lambda bi, hi, i: (bi * nt + i, hi)),
        out_shape=jax.ShapeDtypeStruct((b * t, h * hd), BF16),
        scratch_shapes=[pltpu.VMEM((tq, hd), F32)],
        compiler_params=_params(("arbitrary", "arbitrary", "arbitrary")),
        name=name,
    )(qkv, qkv, qkv, _suffix_matrix(tk))


def _samp_softmax_body(q_ref, kn_ref, vn_ref, kc_ref, vc_ref, bc_ref, bn_ref, *rest, hd, diff, lam_init):
    o_ref = rest[-1]
    t = q_ref.shape[0]
    reps = 2 if diff else 1
    for hh in range(SUBLANES):
        cols = slice(hh * hd, (hh + 1) * hd)
        q = q_ref[:, cols]
        qq = _split_halves(q) if diff else q

        def scores(k, bias):
            s = lax.dot_general(qq, k, _NT, preferred_element_type=F32)
            return (s.reshape(reps, t, -1) + bias[None]).reshape(reps * t, -1)

        s_c = scores(kc_ref[:, hh, :].astype(BF16), bc_ref[hh])
        s_n = scores(kn_ref[:, cols], bn_ref[hh])
        m = jnp.maximum(jnp.max(s_c, axis=-1, keepdims=True), jnp.max(s_n, axis=-1, keepdims=True))
        p_c = jnp.exp2(s_c - m)
        p_n = jnp.exp2(s_n - m)
        l = jnp.sum(p_c, axis=-1, keepdims=True) + jnp.sum(p_n, axis=-1, keepdims=True)
        acc = (jnp.dot(p_c.astype(BF16), vc_ref[:, hh, :].astype(BF16), preferred_element_type=F32)
               + jnp.dot(p_n.astype(BF16), vn_ref[:, cols], preferred_element_type=F32))
        o = acc / l
        if diff:
            o = _diff_finalize(o, rest[0], rest[1], lam_init)
        o_ref[:, cols] = o.astype(BF16)


def _sample_specs(cache_k, cache_v, layer, bs, t, h, hd):
    assert h % SUBLANES == 0
    groups = h // SUBLANES
    width = SUBLANES * hd
    r = cache_k.shape[2]
    ck = cache_k.reshape(cache_k.shape[0], bs, r, groups, SUBLANES, hd)
    cv = cache_v.reshape(cache_v.shape[0], bs, r, groups, SUBLANES, hd)
    cache_spec = pl.BlockSpec((None, None, r, None, SUBLANES, hd), lambda bi, gi: (layer, bi, 0, gi, 0, 0))
    specs = [pl.BlockSpec((t, width), lambda bi, gi: (bi, gi)),
             pl.BlockSpec((t, width), lambda bi, gi: (bi, groups + gi)),
             pl.BlockSpec((t, width), lambda bi, gi: (bi, 2 * groups + gi)),
             cache_spec, cache_spec]
    out_spec = pl.BlockSpec((t, width), lambda bi, gi: (bi, gi))
    return (bs, groups), specs, out_spec, ck, cv


def _attn_softmax_sample(qkv, cache_k, cache_v, layer, bias, bs, t, h, hd, name, diff_args=None):
    r = cache_k.shape[2]
    grid, in_specs, out_spec, ck, cv = _sample_specs(cache_k, cache_v, layer, bs, t, h, hd)
    in_specs += [pl.BlockSpec((SUBLANES, t, r), lambda bi, gi: (gi, 0, 0)),
                 pl.BlockSpec((SUBLANES, t, t), lambda bi, gi: (gi, 0, 0))]
    args = [qkv, qkv, qkv, ck, cv, bias[:, :, :r], bias[:, :, r:]]
    lam_init = None
    if diff_args is not None:
        lam_rows, gain, lam_init = diff_args
        in_specs += [pl.BlockSpec(lam_rows.shape, lambda bi, gi: (0, 0)),
                     pl.BlockSpec((1, hd), lambda bi, gi: (0, 0))]
        args += [lam_rows, gain.reshape(1, hd)]
    return pl.pallas_call(
        functools.partial(_samp_softmax_body, hd=hd, diff=diff_args is not None, lam_init=lam_init),
        grid=grid,
        in_specs=in_specs,
        out_specs=out_spec,
        out_shape=jax.ShapeDtypeStruct((bs * t, h * hd), BF16),
        compiler_params=_params(("arbitrary", "arbitrary")),
        name=name,
    )(*args)


def _samp_stick_body(q_ref, kn_ref, vn_ref, kc_ref, vc_ref, un_ref, u_ref, o_ref, *, tk, hd):
    t = q_ref.shape[0]
    valid = lax.broadcasted_iota(jnp.int32, (t, t), 1) < lax.broadcasted_iota(jnp.int32, (t, t), 0)
    for hh in range(SUBLANES):
        cols = slice(hh * hd, (hh + 1) * hd)
        q = q_ref[:, cols]
        acc, run = _stick_tile(q, kn_ref[:, cols], vn_ref[:, cols], un_ref[...], jnp.zeros((t, 1), F32), valid)
        for tile in reversed(range(kc_ref.shape[0] // tk)):
            rows = slice(tile * tk, (tile + 1) * tk)
            pv, run = _stick_tile(q, kc_ref[rows, hh, :].astype(BF16), vc_ref[rows, hh, :].astype(BF16),
                                  u_ref[...], run, None)
            acc = acc + pv
        o_ref[:, cols] = acc.astype(BF16)


def _attn_stick_sample(qkv, cache_k, cache_v, layer, bs, t, h, hd, name):
    r = cache_k.shape[2]
    tk = MXU_DIM if r % MXU_DIM == 0 else LANES
    assert r % tk == 0
    grid, in_specs, out_spec, ck, cv = _sample_specs(cache_k, cache_v, layer, bs, t, h, hd)
    in_specs += [pl.BlockSpec((t, t), lambda bi, gi: (0, 0)),
                 pl.BlockSpec((tk, tk), lambda bi, gi: (0, 0))]
    return pl.pallas_call(
        functools.partial(_samp_stick_body, tk=tk, hd=hd),
        grid=grid,
        in_specs=in_specs,
        out_specs=out_spec,
        out_shape=jax.ShapeDtypeStruct((bs * t, h * hd), BF16),
        compiler_params=_params(("arbitrary", "arbitrary")),
        name=name,
    )(qkv, qkv, qkv, ck, cv, _suffix_matrix(t), _suffix_matrix(tk))


def _band_bias_sample(table, past_len, r, t):
    qpos = past_len + np.arange(t)
    kpos = past_len - r + np.arange(r + t)
    qc, kc = qpos[:, None] // CHUNK, kpos[None, :] // CHUNK
    valid = (kc <= qc) & (kc >= qc - BAND_CHUNKS)
    vec = _clipped_bias_vec(table, np.arange(r + 2 * t - 1) - (t - 1) + (kpos[0] - qpos[0]))
    return jnp.where(valid[None], _toeplitz(vec, t, r + t), NEG_INF)


def _t5_bias_sample(table, r, t):
    qpos = r + np.arange(t)
    kpos = np.arange(r + t)
    valid = (kpos[None, :] // CHUNK) <= (qpos[:, None] // CHUNK)
    vec = _t5_bias_vec(table, np.arange(r + 2 * t - 1) - (t - 1) + (kpos[0] - qpos[0]))
    return jnp.where(valid[None], _toeplitz(vec, t, r + t), NEG_INF)


def kernel(x_prompt, x_sample, cache_a_k, cache_a_v, cache_b_k, cache_b_v, cache_c_k, cache_c_v, w_in_a, w_out_a, rel_bias_a, w_in_b, w_out_b, lambda_q1, lambda_k1, lambda_q2, lambda_k2, diff_norm_g, t5_bias, w_in_c, w_out_c, ln1_g, ln1_b, ln2_g, ln2_b, w_gate, w_up, w_down):
    b, t, d = x_prompt.shape
    bs, ts, _ = x_sample.shape
    h, hd = cache_a_k.shape[3], cache_a_k.shape[4]
    depth = ln1_g.shape[0]
    alpha = (2 * depth) ** 0.25
    past_len = cache_b_k.shape[2]

    yp = x_prompt.reshape(b * t, d)
    ys = x_sample.reshape(bs * ts, d)
    yp_b, ys_b = yp.astype(BF16), ys.astype(BF16)
    new = {key: [] for key in ("pa_k", "pa_v", "pb_k", "pb_v", "pc_k", "pc_v",
                               "sa_k", "sa_v", "sb_k", "sb_v", "sc_k", "sc_v")}
    for i in range(depth):
        kind, j = i % N_MIXERS, i // N_MIXERS
        w_in = (w_in_a, w_in_b, w_in_c)[kind][j].astype(BF16)
        w_out = (w_out_a, w_out_b, w_out_c)[kind][j].astype(BF16)
        q_scale = ((hd // 2) ** -0.5 if kind == 1 else hd ** -0.5) * LOG2E
        qkv_p, kp, vp = _qkv_proj(yp_b, w_in, q_scale, hd, kind != 0, f"qkv_prompt_{i}")
        qkv_s, ks, vs = _qkv_proj(ys_b, w_in, q_scale, hd, True, f"qkv_sample_{i}")
        kp, vp = kp.reshape(b, t, h, hd), vp.reshape(b, t, h, hd)
        ks, vs = ks.reshape(bs, ts, h, hd), vs.reshape(bs, ts, h, hd)
        if kind == 0:
            op = _attn_a_prompt(qkv_p, rel_bias_a[j], b, t, h, hd, f"band_prompt_{i}")
            r = cache_a_k.shape[2]
            os_ = _attn_softmax_sample(qkv_s, cache_a_k, cache_a_v, j,
                                       _band_bias_sample(rel_bias_a[j], past_len, r, ts),
                                       bs, ts, h, hd, f"band_sample_{i}")
            keep = min(BAND_PAST, t)
            kp, vp = kp[:, -keep:], vp[:, -keep:]
            tag = "a"
        elif kind == 1:
            lam_init = 0.8 - 0.6 * math.exp(-0.3 * i)
            lam_rows = jnp.stack([lambda_q1[j], lambda_k1[j], lambda_q2[j], lambda_k2[j]]).astype(F32)
            op = _attn_b_prompt(qkv_p, t5_bias, lam_rows, diff_norm_g[j], lam_init, b, t, h, hd,
                                f"diff_prompt_{i}")
            r = cache_b_k.shape[2]
            os_ = _attn_softmax_sample(qkv_s, cache_b_k, cache_b_v, j, _t5_bias_sample(t5_bias, r, ts),
                                       bs, ts, h, hd, f"diff_sample_{i}",
                                       diff_args=(lam_rows, diff_norm_g[j], lam_init))
            tag = "b"
        else:
            op = _attn_c_prompt(qkv_p, b, t, h, hd, f"stick_prompt_{i}")
            os_ = _attn_stick_sample(qkv_s, cache_c_k, cache_c_v, j, bs, ts, h, hd, f"stick_sample_{i}")
            tag = "c"
        new[f"p{tag}_k"].append(kp)
        new[f"p{tag}_v"].append(vp)
        new[f"s{tag}_k"].append(ks)
        new[f"s{tag}_v"].append(vs)
        yp = _outproj_ln(op, yp, w_out, ln1_g[i], ln1_b[i], alpha, f"outproj_prompt_{i}")
        ys = _outproj_ln(os_, ys, w_out, ln1_g[i], ln1_b[i], alpha, f"outproj_sample_{i}")
        wg, wu, wd = w_gate[i].astype(BF16), w_up[i].astype(BF16), w_down[i].astype(BF16)
        yp, yp_b = _ffn_ln(yp, wg, wu, wd, ln2_g[i], ln2_b[i], alpha, f"ffn_prompt_{i}")
        ys, ys_b = _ffn_ln(ys, wg, wu, wd, ln2_g[i], ln2_b[i], alpha, f"ffn_sample_{i}")
    return (yp.reshape(b, t, d), ys.reshape(bs, ts, d),
            jnp.stack(new["pa_k"]), jnp.stack(new["pa_v"]), jnp.stack(new["pb_k"]), jnp.stack(new["pb_v"]),
            jnp.stack(new["pc_k"]), jnp.stack(new["pc_v"]), jnp.stack(new["sa_k"]), jnp.stack(new["sa_v"]),
            jnp.stack(new["sb_k"]), jnp.stack(new["sb_v"]), jnp.stack(new["sc_k"]), jnp.stack(new["sc_v"]))
```

```python
import functools
import math

import jax
import jax.numpy as jnp
import numpy as np
from jax import lax
from jax.experimental import pallas as pl
from jax.experimental.pallas import tpu as pltpu

F32 = jnp.float32
BF16 = jnp.bfloat16

CHUNK = 64
N_MIXERS = 3
BAND_CHUNKS = 8
BAND_PAST = BAND_CHUNKS * CHUNK
REL_CLIP_A = 256
T5_BUCKETS = 32
T5_MAX_DIST = 128
LN_EPS = 1e-5
RMS_EPS = 1e-5
NEG_INF = -1e30
LOG2E = math.log2(math.e)

LANES = 128
SUBLANES = 8
MXU_DIM = 256
V7X_VMEM_LIMIT_BYTES = 56 * 1024 * 1024

_NT = (((1,), (1,)), ((), ()))


def _params(semantics):
    return pltpu.CompilerParams(dimension_semantics=semantics, vmem_limit_bytes=V7X_VMEM_LIMIT_BYTES)


def _tiles(m, t=None):
    tm = next(c for c in (1024, 512, m) if m % c == 0)
    cfg = dict(tm=tm, tm_ln=min(tm, 512), tf=512)
    if t is not None:
        cfg.update(tq_band=min(256, t), band_heads=4, tq_diff=min(512, t), tq_stick=min(512, t),
                   tk_stick=MXU_DIM)
    return cfg


def _layer_norm(y, g, b):
    yc = y - jnp.mean(y, axis=-1, keepdims=True)
    var = jnp.mean(yc * yc, axis=-1, keepdims=True)
    return yc * lax.rsqrt(var + LN_EPS) * g + b


def _qkv_body(x_ref, w_ref, qkv_ref, k32_ref, v32_ref, *, nq, hd, q_scale, head_major):
    j = pl.program_id(1)

    def proj():
        return jnp.dot(x_ref[...], w_ref[...], preferred_element_type=F32)

    def store_f32(ref, acc):
        if head_major:
            for hh in range(SUBLANES):
                ref[:, hh, :] = acc[:, hh * hd:(hh + 1) * hd]
        else:
            ref[...] = acc

    @pl.when(j < nq)
    def _():
        qkv_ref[...] = (proj() * q_scale).astype(BF16)

    @pl.when((j >= nq) & (j < 2 * nq))
    def _():
        acc = proj()
        qkv_ref[...] = acc.astype(BF16)
        store_f32(k32_ref, acc)

    @pl.when(j >= 2 * nq)
    def _():
        acc = proj()
        qkv_ref[...] = acc.astype(BF16)
        store_f32(v32_ref, acc)


def _qkv_proj(x, w, q_scale, hd, head_major, name):
    m, d = x.shape
    tm = _tiles(m)["tm"]
    tn = SUBLANES * hd
    nq = d // tn

    def f32_spec(first):
        if head_major:
            return pl.BlockSpec((tm, None, SUBLANES, hd), lambda i, j: (i, jnp.clip(j - first, 0, nq - 1), 0, 0))
        return pl.BlockSpec((tm, tn), lambda i, j: (i, jnp.clip(j - first, 0, nq - 1)))

    f32_shape = jax.ShapeDtypeStruct((m, nq, SUBLANES, hd) if head_major else (m, d), F32)
    qkv, k32, v32 = pl.pallas_call(
        functools.partial(_qkv_body, nq=nq, hd=hd, q_scale=q_scale, head_major=head_major),
        grid=(m // tm, 3 * nq),
        in_specs=[pl.BlockSpec((tm, d), lambda i, j: (i, 0)),
                  pl.BlockSpec((d, tn), lambda i, j: (0, j))],
        out_specs=[pl.BlockSpec((tm, tn), lambda i, j: (i, j)), f32_spec(nq), f32_spec(2 * nq)],
        out_shape=[jax.ShapeDtypeStruct((m, 3 * d), BF16), f32_shape, f32_shape],
        compiler_params=_params(("arbitrary", "arbitrary")),
        name=name,
    )(x, w)
    return qkv, k32.reshape(m, nq * SUBLANES, hd), v32.reshape(m, nq * SUBLANES, hd)


def _outproj_body(o_ref, x_ref, w_ref, g_ref, b_ref, y_ref, *, alpha):
    y = alpha * x_ref[...] + jnp.dot(o_ref[...], w_ref[...], preferred_element_type=F32)
    y_ref[...] = _layer_norm(y, g_ref[...], b_ref[...])


def _outproj_ln(o, x, w, g, b, alpha, name):
    m, d = x.shape
    tm = _tiles(m)["tm_ln"]
    row = pl.BlockSpec((tm, d), lambda i: (i, 0))
    vec = pl.BlockSpec((1, d), lambda i: (0, 0))
    return pl.pallas_call(
        functools.partial(_outproj_body, alpha=alpha),
        grid=(m // tm,),
        in_specs=[row, row, pl.BlockSpec((d, d), lambda i: (0, 0)), vec, vec],
        out_specs=row,
        out_shape=jax.ShapeDtypeStruct((m, d), F32),
        compiler_params=_params(("arbitrary",)),
        name=name,
    )(o, x, w, g.reshape(1, d), b.reshape(1, d))


def _ffn_body(x_ref, wg_ref, wu_ref, wd_ref, g_ref, b_ref, y_ref, yb_ref, xb_ref, acc_ref, *, alpha):
    f = pl.program_id(1)

    @pl.when(f == 0)
    def _():
        xb_ref[...] = x_ref[...].astype(BF16)
        acc_ref[...] = jnp.zeros_like(acc_ref)

    xb = xb_ref[...]
    gate = jnp.dot(xb, wg_ref[...], preferred_element_type=F32)
    up = jnp.dot(xb, wu_ref[...], preferred_element_type=F32)
    h = gate * (1.0 / (1.0 + jnp.exp(-gate))) * up
    acc_ref[...] += jnp.dot(h.astype(BF16), wd_ref[...], preferred_element_type=F32)

    @pl.when(f == pl.num_programs(1) - 1)
    def _():
        y = _layer_norm(alpha * x_ref[...] + acc_ref[...], g_ref[...], b_ref[...])
        y_ref[...] = y
        yb_ref[...] = y.astype(BF16)


def _ffn_ln(x, wg, wu, wd, g, b, alpha, name):
    m, d = x.shape
    dff = wg.shape[1]
    t = _tiles(m)
    tm, tf = t["tm_ln"], t["tf"]
    row = pl.BlockSpec((tm, d), lambda i, f: (i, 0))
    vec = pl.BlockSpec((1, d), lambda i, f: (0, 0))
    return pl.pallas_call(
        functools.partial(_ffn_body, alpha=alpha),
        grid=(m // tm, dff // tf),
        in_specs=[row,
                  pl.BlockSpec((d, tf), lambda i, f: (0, f)),
                  pl.BlockSpec((d, tf), lambda i, f: (0, f)),
                  pl.BlockSpec((tf, d), lambda i, f: (f, 0)),
                  vec, vec],
        out_specs=[row, row],
        out_shape=[jax.ShapeDtypeStruct((m, d), F32), jax.ShapeDtypeStruct((m, d), BF16)],
        scratch_shapes=[pltpu.VMEM((tm, d), BF16), pltpu.VMEM((tm, d), F32)],
        compiler_params=_params(("arbitrary", "arbitrary")),
        name=name,
    )(x, wg, wu, wd, g.reshape(1, d), b.reshape(1, d))


def _toeplitz(vec, rows, cols):
    hh, length = vec.shape
    assert length == rows + cols - 1
    padded = jnp.pad(vec, ((0, 0), (0, 1)))
    skew = jnp.tile(padded, (1, rows))[:, :rows * length].reshape(hh, rows, length)
    return skew[:, :, rows - 1:]


def _split_halves(q):
    lane = lax.broadcasted_iota(jnp.int32, q.shape, 1)
    half = q.shape[1] // 2
    zero = jnp.zeros_like(q)
    return jnp.concatenate([jnp.where(lane < half, q, zero), jnp.where(lane >= half, q, zero)], axis=0)


def _diff_finalize(o, lam_ref, g_ref, lam_init):
    t = o.shape[0] // 2
    lp = lam_ref[...]
    lam = (jnp.exp(jnp.sum(lp[0:1] * lp[1:2], axis=-1, keepdims=True))
           - jnp.exp(jnp.sum(lp[2:3] * lp[3:4], axis=-1, keepdims=True)) + lam_init)
    of = o[:t] - lam * o[t:]
    of = of * lax.rsqrt(jnp.mean(of * of, axis=-1, keepdims=True) + RMS_EPS) * g_ref[...]
    return of * (1.0 - lam_init)


def _neg_abs(z):
    return pltpu.bitcast(pltpu.bitcast(z, jnp.uint32) | jnp.uint32(0x80000000), F32)


def _suffix_matrix(n):
    return jnp.asarray(np.arange(n)[:, None] > np.arange(n)[None, :], BF16)


def _stick_tile(q, kt, vt, u, run, valid):
    z = lax.dot_general(q, kt, _NT, preferred_element_type=F32)
    log_beta = jnp.minimum(z, 0.0) - jnp.log(1.0 + jnp.exp2(_neg_abs(z))) * LOG2E
    log_1m = log_beta - z
    if valid is not None:
        log_1m = jnp.where(valid, log_1m, 0.0)
    suffix = jnp.dot(log_1m.astype(BF16), u, preferred_element_type=F32)
    a = jnp.exp2(log_beta + suffix + run)
    if valid is not None:
        a = jnp.where(valid, a, 0.0)
    return (jnp.dot(a.astype(BF16), vt, preferred_element_type=F32),
            run + jnp.sum(log_1m, axis=-1, keepdims=True))


def _attn_a_body(q_ref, k_ref, v_ref, bias_ref, o_ref, *, tq, nb, heads, hd):
    i = pl.program_id(2)
    starts = [pl.multiple_of(jnp.maximum(i - (nb - 1) + j, 0) * tq, tq) for j in range(nb)]
    for hh in range(heads):
        cols = slice(hh * hd, (hh + 1) * hd)
        q = q_ref[:, cols]
        scores = []
        for j in range(nb):
            s = lax.dot_general(q, k_ref[pl.ds(starts[j], tq), cols], _NT, preferred_element_type=F32)
            s = s + bias_ref[hh, :, j * tq:(j + 1) * tq]
            if j < nb - 1:
                s = jnp.where(i - (nb - 1) + j >= 0, s, NEG_INF)
            scores.append(s)
        m = functools.reduce(jnp.maximum, [jnp.max(s, axis=-1, keepdims=True) for s in scores])
        l = jnp.zeros_like(m)
        acc = jnp.zeros((tq, hd), F32)
        for start, s in zip(starts, scores):
            p = jnp.exp2(s - m)
            l = l + jnp.sum(p, axis=-1, keepdims=True)
            acc = acc + jnp.dot(p.astype(BF16), v_ref[pl.ds(start, tq), cols], preferred_element_type=F32)
        o_ref[:, cols] = (acc / l).astype(BF16)


def _clipped_bias_vec(table, rel):
    idx = np.clip(rel, -REL_CLIP_A, REL_CLIP_A) + REL_CLIP_A
    return jnp.transpose(table[idx]).astype(F32) * LOG2E


def _band_bias_prompt(table, tq, nb):
    w = nb * tq
    r = np.arange(tq)[:, None]
    off = np.arange(w)[None, :] - (nb - 1) * tq
    kc, qc = off // CHUNK, r // CHUNK
    valid = (kc <= qc) & (kc >= qc - BAND_CHUNKS)
    vec = _clipped_bias_vec(table, np.arange(tq + w - 1) - (tq - 1) - (nb - 1) * tq)
    return jnp.where(valid[None], _toeplitz(vec, tq, w), NEG_INF)


def _attn_a_prompt(qkv, table, b, t, h, hd, name):
    cfg = _tiles(b * t, t)
    tq, heads = cfg["tq_band"], cfg["band_heads"]
    assert BAND_PAST % tq == 0 and t % tq == 0 and tq % CHUNK == 0 and h % heads == 0
    nb = BAND_PAST // tq + 1
    nt = t // tq
    hg = h // heads
    bias = _band_bias_prompt(table, tq, nb)
    return pl.pallas_call(
        functools.partial(_attn_a_body, tq=tq, nb=nb, heads=heads, hd=hd),
        grid=(b, hg, nt),
        in_specs=[pl.BlockSpec((tq, heads * hd), lambda bi, gi, i: (bi * nt + i, gi)),
                  pl.BlockSpec((t, heads * hd), lambda bi, gi, i: (bi, hg + gi)),
                  pl.BlockSpec((t, heads * hd), lambda bi, gi, i: (bi, 2 * hg + gi)),
                  pl.BlockSpec((heads, tq, nb * tq), lambda bi, gi, i: (gi, 0, 0))],
        out_specs=pl.BlockSpec((tq, heads * hd), lambda bi, gi, i: (bi * nt + i, gi)),
        out_shape=jax.ShapeDtypeStruct((b * t, h * hd), BF16),
        compiler_params=_params(("arbitrary", "arbitrary", "arbitrary")),
        name=name,
    )(qkv, qkv, qkv, bias)


def _lane_fold(x, op):
    return functools.reduce(op, [x[:, c * LANES:(c + 1) * LANES] for c in range(x.shape[1] // LANES)])


STABILISER_SLACK = 64.0


def _attn_b_body(q_ref, k_ref, v_ref, near_ref, far_ref, lam_ref, g_ref, o_ref, acc_ref, stab_ref, l_ref,
                 track_ref, near_s_ref, *, tq, lam_init):
    i = pl.program_id(2)
    qq = _split_halves(q_ref[...])
    far = far_ref[:, 0:1]
    n_far = jnp.maximum(i - 1, 0)

    def far_scores(j):
        start = pl.multiple_of(j * tq, tq)
        return lax.dot_general(qq, k_ref[pl.ds(start, tq), :], _NT, preferred_element_type=F32), start

    def near_scores(j):
        s, start = far_scores(j)
        return (s.reshape(2, tq, tq) + near_ref[j - i + 1][None]).reshape(2 * tq, tq), start

    def near_max():
        track_ref[...] = jnp.full_like(track_ref, NEG_INF)

        def body(j, carry):
            s = near_scores(j)[0]
            near_s_ref[j - n_far] = s
            track_ref[...] = jnp.maximum(track_ref[...], _lane_fold(s, jnp.maximum))
            return carry

        lax.fori_loop(n_far, i + 1, body, 0)
        return jnp.max(track_ref[...], axis=-1, keepdims=True)

    def cached_near_scores(j):
        return near_s_ref[j - n_far], pl.multiple_of(j * tq, tq)

    def sum_pass(scores, shift, lo, hi, track):
        def body(j, carry):
            s, start = scores(j)
            if track:
                track_ref[...] = jnp.maximum(track_ref[...], _lane_fold(s, jnp.maximum))
            p = jnp.exp2(s - jnp.tile(stab_ref[...] - shift, (1, tq // LANES)))
            l_ref[...] += _lane_fold(p, jnp.add)
            acc_ref[...] += jnp.dot(p.astype(BF16), v_ref[pl.ds(start, tq), :], preferred_element_type=F32)
            return carry

        lax.fori_loop(lo, hi, body, 0)

    def sum_all(m, track):
        stab_ref[...] = jnp.broadcast_to(m, stab_ref.shape)
        l_ref[...] = jnp.zeros_like(l_ref)
        acc_ref[...] = jnp.zeros_like(acc_ref)
        sum_pass(far_scores, far, 0, n_far, track)
        sum_pass(cached_near_scores, 0.0, n_far, i + 1, False)

    def finish():
        l = jnp.sum(l_ref[...], axis=-1, keepdims=True)
        o_ref[...] = _diff_finalize(acc_ref[...] / l, lam_ref, g_ref, lam_init).astype(BF16)

    m_near = near_max()
    track_ref[...] = jnp.full_like(track_ref, NEG_INF)
    sum_all(m_near, True)
    m_far = jnp.max(track_ref[...], axis=-1, keepdims=True) + far
    within_slack = jnp.max(m_far - m_near) <= STABILISER_SLACK

    @pl.when(within_slack)
    def _():
        finish()

    @pl.when(jnp.logical_not(within_slack))
    def _():
        sum_all(jnp.maximum(m_far, m_near), False)
        finish()


def _t5_bucket(rel):
    half = T5_BUCKETS // 2
    max_exact = half // 2
    n = jnp.abs(rel)
    nf = jnp.maximum(n, 1).astype(F32)
    large = max_exact + (jnp.log(nf / max_exact) / math.log(T5_MAX_DIST / max_exact)
                         * (half - max_exact)).astype(jnp.int32)
    return jnp.where(rel > 0, half, 0) + jnp.where(n < max_exact, n, jnp.minimum(large, half - 1))


def _t5_far_distance():
    half = T5_BUCKETS // 2
    max_exact = half // 2
    return math.ceil(max_exact * (T5_MAX_DIST / max_exact) ** ((half - 1 - max_exact + 0.5) / (half - max_exact)))


def _t5_bias_vec(table, rel):
    return jnp.transpose(table[_t5_bucket(jnp.asarray(rel, jnp.int32))]).astype(F32) * LOG2E


def _t5_bias_prompt(table, tq):
    assert tq + 1 >= _t5_far_distance()
    r = np.arange(tq)[:, None]
    c = np.arange(tq)[None, :]
    left = _toeplitz(_t5_bias_vec(table, np.arange(2 * tq - 1) - (tq - 1) - tq), tq, tq)
    diag = _toeplitz(_t5_bias_vec(table, np.arange(2 * tq - 1) - (tq - 1)), tq, tq)
    diag = jnp.where(((c // CHUNK) <= (r // CHUNK))[None], diag, NEG_INF)
    far = jnp.broadcast_to(_t5_bias_vec(table, np.array([-(tq + 1)]))[:, :, None], (table.shape[1], 1, LANES))
    return jnp.stack([left, diag], axis=1), far


def _attn_b_prompt(qkv, table, lam_rows, gain, lam_init, b, t, h, hd, name):
    tq = _tiles(b * t, t)["tq_diff"]
    assert t % tq == 0 and tq % CHUNK == 0
    nt = t // tq
    near, far = _t5_bias_prompt(table, tq)
    return pl.pallas_call(
        functools.partial(_attn_b_body, tq=tq, lam_init=lam_init),
        grid=(b, h, nt),
        in_specs=[pl.BlockSpec((tq, hd), lambda bi, hi, i: (bi * nt + i, hi)),
                  pl.BlockSpec((t, hd), lambda bi, hi, i: (bi, h + hi)),
                  pl.BlockSpec((t, hd), lambda bi, hi, i: (bi, 2 * h + hi)),
                  pl.BlockSpec((None, 2, tq, tq), lambda bi, hi, i: (hi, 0, 0, 0)),
                  pl.BlockSpec((None, 1, LANES), lambda bi, hi, i: (hi, 0, 0)),
                  pl.BlockSpec(lam_rows.shape, lambda bi, hi, i: (0, 0)),
                  pl.BlockSpec((1, hd), lambda bi, hi, i: (0, 0))],
        out_specs=pl.BlockSpec((tq, hd), lambda bi, hi, i: (bi * nt + i, hi)),
        out_shape=jax.ShapeDtypeStruct((b * t, h * hd), BF16),
        scratch_shapes=([pltpu.VMEM((2 * tq, hd), F32)] + [pltpu.VMEM((2 * tq, LANES), F32)] * 3
                        + [pltpu.VMEM((2, 2 * tq, tq), F32)]),
        compiler_params=_params(("arbitrary", "arbitrary", "arbitrary")),
        name=name,
    )(qkv, qkv, qkv, near, far, lam_rows, gain.reshape(1, hd))


def _attn_c_body(q_ref, k_ref, v_ref, u_ref, o_ref, acc_ref, *, tq, tk):
    i = pl.program_id(2)
    q = q_ref[...]
    u = u_ref[...]
    subs = tq // tk
    below_diag = (lax.broadcasted_iota(jnp.int32, (tk, tk), 1) < lax.broadcasted_iota(jnp.int32, (tk, tk), 0))

    def group(base, run):
        total = None
        for sub in reversed(range(subs)):
            start = pl.multiple_of(base + sub * tk, tk)
            pv, run = _stick_tile(q, k_ref[pl.ds(start, tk), :], v_ref[pl.ds(start, tk), :], u, run, None)
            total = pv if total is None else total + pv
        return total, run

    runs = []
    for rb in range(subs):
        rows = slice(rb * tk, (rb + 1) * tk)
        total, run = None, jnp.zeros((tk, 1), F32)
        for sub in reversed(range(rb + 1)):
            start = pl.multiple_of(i * tq + sub * tk, tk)
            pv, run = _stick_tile(q[rows], k_ref[pl.ds(start, tk), :], v_ref[pl.ds(start, tk), :], u, run,
                                  below_diag if sub == rb else None)
            total = pv if total is None else total + pv
        acc_ref[rows, :] = total
        runs.append(run)
    run = jnp.concatenate(runs, axis=0)

    def one_group(base, run):
        pv, run = group(base, run)
        acc_ref[...] += pv
        return run

    run = lax.cond(i % 2 == 1, lambda r: one_group((i - 1) * tq, r), lambda r: r, run)
    first = i - 1 - i % 2

    def body(step, run):
        pv0, run = group((first - 2 * step) * tq, run)
        pv1, run = group((first - 2 * step - 1) * tq, run)
        acc_ref[...] += pv0 + pv1
        return run

    lax.fori_loop(0, i // 2, body, run)
    o_ref[...] = acc_ref[...].astype(BF16)


def _attn_c_prompt(qkv, b, t, h, hd, name):
    cfg = _tiles(b * t, t)
    tq, tk = cfg["tq_stick"], min(cfg["tk_stick"], cfg["tq_stick"])
    assert t % tq == 0 and tq % tk == 0
    nt = t // tq
    return pl.pallas_call(
        functools.partial(_attn_c_body, tq=tq, tk=tk),
        grid=(b, h, nt),
        in_specs=[pl.BlockSpec((tq, hd), lambda bi, hi, i: (bi * nt + i, hi)),
                  pl.BlockSpec((t, hd), lambda bi, hi, i: (bi, h + hi)),
                  pl.BlockSpec((t, hd), lambda bi, hi, i: (bi, 2 * h + hi)),
                  pl.BlockSpec((tk, tk), lambda bi, hi, i: (0, 0))],
        out_specs=pl.BlockSpec((tq, hd), lambda bi, hi, i: (bi * nt + i, hi)),
        out_shape=jax.ShapeDtypeStruct((b * t, h * hd), BF16),
        scratch_shapes=[pltpu.VMEM((tq, hd), F32)],
        compiler_params=_params(("arbitrary", "arbitrary", "arbitrary")),
        name=name,
    )(qkv, qkv, qkv, _suffix_matrix(tk))


def _samp_softmax_body(q_ref, kn_ref, vn_ref, kc_ref, vc_ref, bc_ref, bn_ref, *rest, hd, diff, lam_init):
    o_ref = rest[-1]
    t = q_ref.shape[0]
    reps = 2 if diff else 1
    for hh in range(SUBLANES):
        cols = slice(hh * hd, (hh + 1) * hd)
        q = q_ref[:, cols]
        qq = _split_halves(q) if diff else q

        def scores(k, bias):
            s = lax.dot_general(qq, k, _NT, preferred_element_type=F32)
            return (s.reshape(reps, t, -1) + bias[None]).reshape(reps * t, -1)

        s_c = scores(kc_ref[:, hh, :].astype(BF16), bc_ref[hh])
        s_n = scores(kn_ref[:, cols], bn_ref[hh])
        m = jnp.maximum(jnp.max(s_c, axis=-1, keepdims=True), jnp.max(s_n, axis=-1, keepdims=True))
        p_c = jnp.exp2(s_c - m)
        p_n = jnp.exp2(s_n - m)
        l = jnp.sum(p_c, axis=-1, keepdims=True) + jnp.sum(p_n, axis=-1, keepdims=True)
        acc = (jnp.dot(p_c.astype(BF16), vc_ref[:, hh, :].astype(BF16), preferred_element_type=F32)
               + jnp.dot(p_n.astype(BF16), vn_ref[:, cols], preferred_element_type=F32))
        o = acc / l
        if diff:
            o = _diff_finalize(o, rest[0], rest[1], lam_init)
        o_ref[:, cols] = o.astype(BF16)


def _sample_specs(cache_k, cache_v, layer, bs, t, h, hd):
    assert h % SUBLANES == 0
    groups = h // SUBLANES
    width = SUBLANES * hd
    r = cache_k.shape[2]
    ck = cache_k.reshape(cache_k.shape[0], bs, r, groups, SUBLANES, hd)
    cv = cache_v.reshape(cache_v.shape[0], bs, r, groups, SUBLANES, hd)
    cache_spec = pl.BlockSpec((None, None, r, None, SUBLANES, hd), lambda bi, gi: (layer, bi, 0, gi, 0, 0))
    specs = [pl.BlockSpec((t, width), lambda bi, gi: (bi, gi)),
             pl.BlockSpec((t, width), lambda bi, gi: (bi, groups + gi)),
             pl.BlockSpec((t, width), lambda bi, gi: (bi, 2 * groups + gi)),
             cache_spec, cache_spec]
    out_spec = pl.BlockSpec((t, width), lambda bi, gi: (bi, gi))
    return (bs, groups), specs, out_spec, ck, cv


def _attn_softmax_sample(qkv, cache_k, cache_v, layer, bias, bs, t, h, hd, name, diff_args=None):
    r = cache_k.shape[2]
    grid, in_specs, out_spec, ck, cv = _sample_specs(cache_k, cache_v, layer, bs, t, h, hd)
    in_specs += [pl.BlockSpec((SUBLANES, t, r), lambda bi, gi: (gi, 0, 0)),
                 pl.BlockSpec((SUBLANES, t, t), lambda bi, gi: (gi, 0, 0))]
    args = [qkv, qkv, qkv, ck, cv, bias[:, :, :r], bias[:, :, r:]]
    lam_init = None
    if diff_args is not None:
        lam_rows, gain, lam_init = diff_args
        in_specs += [pl.BlockSpec(lam_rows.shape, lambda bi, gi: (0, 0)),
                     pl.BlockSpec((1, hd), lambda bi, gi: (0, 0))]
        args += [lam_rows, gain.reshape(1, hd)]
    return pl.pallas_call(
        functools.partial(_samp_softmax_body, hd=hd, diff=diff_args is not None, lam_init=lam_init),
        grid=grid,
        in_specs=in_specs,
        out_specs=out_spec,
        out_shape=jax.ShapeDtypeStruct((bs * t, h * hd), BF16),
        compiler_params=_params(("arbitrary", "arbitrary")),
        name=name,
    )(*args)


def _samp_stick_body(q_ref, kn_ref, vn_ref, kc_ref, vc_ref, un_ref, u_ref, o_ref, *, tk, hd):
    t = q_ref.shape[0]
    valid = lax.broadcasted_iota(jnp.int32, (t, t), 1) < lax.broadcasted_iota(jnp.int32, (t, t), 0)
    for hh in range(SUBLANES):
        cols = slice(hh * hd, (hh + 1) * hd)
        q = q_ref[:, cols]
        acc, run = _stick_tile(q, kn_ref[:, cols], vn_ref[:, cols], un_ref[...], jnp.zeros((t, 1), F32), valid)
        for tile in reversed(range(kc_ref.shape[0] // tk)):
            rows = slice(tile * tk, (tile + 1) * tk)
            pv, run = _stick_tile(q, kc_ref[rows, hh, :].astype(BF16), vc_ref[rows, hh, :].astype(BF16),
                                  u_ref[...], run, None)
            acc = acc + pv
        o_ref[:, cols] = acc.astype(BF16)


def _attn_stick_sample(qkv, cache_k, cache_v, layer, bs, t, h, hd, name):
    r = cache_k.shape[2]
    tk = MXU_DIM if r % MXU_DIM == 0 else LANES
    assert r % tk == 0
    grid, in_specs, out_spec, ck, cv = _sample_specs(cache_k, cache_v, layer, bs, t, h, hd)
    in_specs += [pl.BlockSpec((t, t), lambda bi, gi: (0, 0)),
                 pl.BlockSpec((tk, tk), lambda bi, gi: (0, 0))]
    return pl.pallas_call(
        functools.partial(_samp_stick_body, tk=tk, hd=hd),
        grid=grid,
        in_specs=in_specs,
        out_specs=out_spec,
        out_shape=jax.ShapeDtypeStruct((bs * t, h * hd), BF16),
        compiler_params=_params(("arbitrary", "arbitrary")),
        name=name,
    )(qkv, qkv, qkv, ck, cv, _suffix_matrix(t), _suffix_matrix(tk))


def _band_bias_sample(table, past_len, r, t):
    qpos = past_len + np.arange(t)
    kpos = past_len - r + np.arange(r + t)
    qc, kc = qpos[:, None] // CHUNK, kpos[None, :] // CHUNK
    valid = (kc <= qc) & (kc >= qc - BAND_CHUNKS)
    vec = _clipped_bias_vec(table, np.arange(r + 2 * t - 1) - (t - 1) + (kpos[0] - qpos[0]))
    return jnp.where(valid[None], _toeplitz(vec, t, r + t), NEG_INF)


def _t5_bias_sample(table, r, t):
    qpos = r + np.arange(t)
    kpos = np.arange(r + t)
    valid = (kpos[None, :] // CHUNK) <= (qpos[:, None] // CHUNK)
    vec = _t5_bias_vec(table, np.arange(r + 2 * t - 1) - (t - 1) + (kpos[0] - qpos[0]))
    return jnp.where(valid[None], _toeplitz(vec, t, r + t), NEG_INF)


def kernel(x_prompt, x_sample, cache_a_k, cache_a_v, cache_b_k, cache_b_v, cache_c_k, cache_c_v, w_in_a, w_out_a, rel_bias_a, w_in_b, w_out_b, lambda_q1, lambda_k1, lambda_q2, lambda_k2, diff_norm_g, t5_bias, w_in_c, w_out_c, ln1_g, ln1_b, ln2_g, ln2_b, w_gate, w_up, w_down):
    b, t, d = x_prompt.shape
    bs, ts, _ = x_sample.shape
    h, hd = cache_a_k.shape[3], cache_a_k.shape[4]
    depth = ln1_g.shape[0]
    alpha = (2 * depth) ** 0.25
    past_len = cache_b_k.shape[2]

    yp = x_prompt.reshape(b * t, d)
    ys = x_sample.reshape(bs * ts, d)
    yp_b, ys_b = yp.astype(BF16), ys.astype(BF16)
    new = {key: [] for key in ("pa_k", "pa_v", "pb_k", "pb_v", "pc_k", "pc_v",
                               "sa_k", "sa_v", "sb_k", "sb_v", "sc_k", "sc_v")}
    for i in range(depth):
        kind, j = i % N_MIXERS, i // N_MIXERS
        w_in = (w_in_a, w_in_b, w_in_c)[kind][j].astype(BF16)
        w_out = (w_out_a, w_out_b, w_out_c)[kind][j].astype(BF16)
        q_scale = ((hd // 2) ** -0.5 if kind == 1 else hd ** -0.5) * LOG2E
        qkv_p, kp, vp = _qkv_proj(yp_b, w_in, q_scale, hd, kind != 0, f"qkv_prompt_{i}")
        qkv_s, ks, vs = _qkv_proj(ys_b, w_in, q_scale, hd, True, f"qkv_sample_{i}")
        kp, vp = kp.reshape(b, t, h, hd), vp.reshape(b, t, h, hd)
        ks, vs = ks.reshape(bs, ts, h, hd), vs.reshape(bs, ts, h, hd)
        if kind == 0:
            op = _attn_a_prompt(qkv_p, rel_bias_a[j], b, t, h, hd, f"band_prompt_{i}")
            r = cache_a_k.shape[2]
            os_ = _attn_softmax_sample(qkv_s, cache_a_k, cache_a_v, j,
                                       _band_bias_sample(rel_bias_a[j], past_len, r, ts),
                                       bs, ts, h, hd, f"band_sample_{i}")
            keep = min(BAND_PAST, t)
            kp, vp = kp[:, -keep:], vp[:, -keep:]
            tag = "a"
        elif kind == 1:
            lam_init = 0.8 - 0.6 * math.exp(-0.3 * i)
            lam_rows = jnp.stack([lambda_q1[j], lambda_k1[j], lambda_q2[j], lambda_k2[j]]).astype(F32)
            op = _attn_b_prompt(qkv_p, t5_bias, lam_rows, diff_norm_g[j], lam_init, b, t, h, hd,
                                f"diff_prompt_{i}")
            r = cache_b_k.shape[2]
            os_ = _attn_softmax_sample(qkv_s, cache_b_k, cache_b_v, j, _t5_bias_sample(t5_bias, r, ts),
                                       bs, ts, h, hd, f"diff_sample_{i}",
                                       diff_args=(lam_rows, diff_norm_g[j], lam_init))
            tag = "b"
        else:
            op = _attn_c_prompt(qkv_p, b, t, h, hd, f"stick_prompt_{i}")
            os_ = _attn_stick_sample(qkv_s, cache_c_k, cache_c_v, j, bs, ts, h, hd, f"stick_sample_{i}")
            tag = "c"
        new[f"p{tag}_k"].append(kp)
        new[f"p{tag}_v"].append(vp)
        new[f"s{tag}_k"].append(ks)
        new[f"s{tag}_v"].append(vs)
        yp = _outproj_ln(op, yp, w_out, ln1_g[i], ln1_b[i], alpha, f"outproj_prompt_{i}")
        ys = _outproj_ln(os_, ys, w_out, ln1_g[i], ln1_b[i], alpha, f"outproj_sample_{i}")
        wg, wu, wd = w_gate[i].astype(BF16), w_up[i].astype(BF16), w_down[i].astype(BF16)
        yp, yp_b = _ffn_ln(yp, wg, wu, wd, ln2_g[i], ln2_b[i], alpha, f"ffn_prompt_{i}")
        ys, ys_b = _ffn_ln(ys, wg, wu, wd, ln2_g[i], ln2_b[i], alpha, f"ffn_sample_{i}")
    return (yp.reshape(b, t, d), ys.reshape(bs, ts, d),
            jnp.stack(new["pa_k"]), jnp.stack(new["pa_v"]), jnp.stack(new["pb_k"]), jnp.stack(new["pb_v"]),
            jnp.stack(new["pc_k"]), jnp.stack(new["pc_v"]), jnp.stack(new["sa_k"]), jnp.stack(new["sa_v"]),
            jnp.stack(new["sb_k"]), jnp.stack(new["sb_v"]), jnp.stack(new["sc_k"]), jnp.stack(new["sc_v"]))
```

```python
import functools
import math

import jax
import jax.numpy as jnp
import numpy as np
from jax import lax
from jax.experimental import pallas as pl
from jax.experimental.pallas import tpu as pltpu

F32 = jnp.float32
BF16 = jnp.bfloat16

CHUNK = 64
N_MIXERS = 3
BAND_CHUNKS = 8
BAND_PAST = BAND_CHUNKS * CHUNK
REL_CLIP_A = 256
T5_BUCKETS = 32
T5_MAX_DIST = 128
LN_EPS = 1e-5
RMS_EPS = 1e-5
NEG_INF = -1e30
LOG2E = math.log2(math.e)

LANES = 128
SUBLANES = 8
MXU_DIM = 256
V7X_VMEM_LIMIT_BYTES = 56 * 1024 * 1024

_NT = (((1,), (1,)), ((), ()))


def _params(semantics):
    return pltpu.CompilerParams(dimension_semantics=semantics, vmem_limit_bytes=V7X_VMEM_LIMIT_BYTES)


def _tiles(m, t=None):
    tm = next(c for c in (1024, 512, m) if m % c == 0)
    cfg = dict(tm=tm, tm_ln=min(tm, 512), tf=512)
    if t is not None:
        cfg.update(tq_band=min(256, t), band_heads=4, tq_diff=min(512, t), tq_stick=min(1024, t),
                   tk_stick=MXU_DIM)
    return cfg


def _layer_norm(y, g, b):
    yc = y - jnp.mean(y, axis=-1, keepdims=True)
    var = jnp.mean(yc * yc, axis=-1, keepdims=True)
    return yc * lax.rsqrt(var + LN_EPS) * g + b


def _qkv_body(x_ref, w_ref, qkv_ref, k32_ref, v32_ref, *, nq, hd, q_scale, head_major):
    j = pl.program_id(1)

    def proj():
        return jnp.dot(x_ref[...], w_ref[...], preferred_element_type=F32)

    def store_f32(ref, acc):
        if head_major:
            for hh in range(SUBLANES):
                ref[:, hh, :] = acc[:, hh * hd:(hh + 1) * hd]
        else:
            ref[...] = acc

    @pl.when(j < nq)
    def _():
        qkv_ref[...] = (proj() * q_scale).astype(BF16)

    @pl.when((j >= nq) & (j < 2 * nq))
    def _():
        acc = proj()
        qkv_ref[...] = acc.astype(BF16)
        store_f32(k32_ref, acc)

    @pl.when(j >= 2 * nq)
    def _():
        acc = proj()
        qkv_ref[...] = acc.astype(BF16)
        store_f32(v32_ref, acc)


def _qkv_proj(x, w, q_scale, hd, head_major, name):
    m, d = x.shape
    tm = _tiles(m)["tm"]
    tn = SUBLANES * hd
    nq = d // tn

    def f32_spec(first):
        if head_major:
            return pl.BlockSpec((tm, None, SUBLANES, hd), lambda i, j: (i, jnp.clip(j - first, 0, nq - 1), 0, 0))
        return pl.BlockSpec((tm, tn), lambda i, j: (i, jnp.clip(j - first, 0, nq - 1)))

    f32_shape = jax.ShapeDtypeStruct((m, nq, SUBLANES, hd) if head_major else (m, d), F32)
    qkv, k32, v32 = pl.pallas_call(
        functools.partial(_qkv_body, nq=nq, hd=hd, q_scale=q_scale, head_major=head_major),
        grid=(m // tm, 3 * nq),
        in_specs=[pl.BlockSpec((tm, d), lambda i, j: (i, 0)),
                  pl.BlockSpec((d, tn), lambda i, j: (0, j))],
        out_specs=[pl.BlockSpec((tm, tn), lambda i, j: (i, j)), f32_spec(nq), f32_spec(2 * nq)],
        out_shape=[jax.ShapeDtypeStruct((m, 3 * d), BF16), f32_shape, f32_shape],
        compiler_params=_params(("arbitrary", "arbitrary")),
        name=name,
    )(x, w)
    return qkv, k32.reshape(m, nq * SUBLANES, hd), v32.reshape(m, nq * SUBLANES, hd)


def _outproj_body(o_ref, x_ref, w_ref, g_ref, b_ref, y_ref, *, alpha):
    y = alpha * x_ref[...] + jnp.dot(o_ref[...], w_ref[...], preferred_element_type=F32)
    y_ref[...] = _layer_norm(y, g_ref[...], b_ref[...])


def _outproj_ln(o, x, w, g, b, alpha, name):
    m, d = x.shape
    tm = _tiles(m)["tm_ln"]
    row = pl.BlockSpec((tm, d), lambda i: (i, 0))
    vec = pl.BlockSpec((1, d), lambda i: (0, 0))
    return pl.pallas_call(
        functools.partial(_outproj_body, alpha=alpha),
        grid=(m // tm,),
        in_specs=[row, row, pl.BlockSpec((d, d), lambda i: (0, 0)), vec, vec],
        out_specs=row,
        out_shape=jax.ShapeDtypeStruct((m, d), F32),
        compiler_params=_params(("arbitrary",)),
        name=name,
    )(o, x, w, g.reshape(1, d), b.reshape(1, d))


def _ffn_body(x_ref, wg_ref, wu_ref, wd_ref, g_ref, b_ref, y_ref, yb_ref, xb_ref, acc_ref, *, alpha):
    f = pl.program_id(1)

    @pl.when(f == 0)
    def _():
        xb_ref[...] = x_ref[...].astype(BF16)
        acc_ref[...] = jnp.zeros_like(acc_ref)

    xb = xb_ref[...]
    gate = jnp.dot(xb, wg_ref[...], preferred_element_type=F32)
    up = jnp.dot(xb, wu_ref[...], preferred_element_type=F32)
    h = gate * (1.0 / (1.0 + jnp.exp(-gate))) * up
    acc_ref[...] += jnp.dot(h.astype(BF16), wd_ref[...], preferred_element_type=F32)

    @pl.when(f == pl.num_programs(1) - 1)
    def _():
        y = _layer_norm(alpha * x_ref[...] + acc_ref[...], g_ref[...], b_ref[...])
        y_ref[...] = y
        yb_ref[...] = y.astype(BF16)


def _ffn_ln(x, wg, wu, wd, g, b, alpha, name):
    m, d = x.shape
    dff = wg.shape[1]
    t = _tiles(m)
    tm, tf = t["tm_ln"], t["tf"]
    row = pl.BlockSpec((tm, d), lambda i, f: (i, 0))
    vec = pl.BlockSpec((1, d), lambda i, f: (0, 0))
    return pl.pallas_call(
        functools.partial(_ffn_body, alpha=alpha),
        grid=(m // tm, dff // tf),
        in_specs=[row,
                  pl.BlockSpec((d, tf), lambda i, f: (0, f)),
                  pl.BlockSpec((d, tf), lambda i, f: (0, f)),
                  pl.BlockSpec((tf, d), lambda i, f: (f, 0)),
                  vec, vec],
        out_specs=[row, row],
        out_shape=[jax.ShapeDtypeStruct((m, d), F32), jax.ShapeDtypeStruct((m, d), BF16)],
        scratch_shapes=[pltpu.VMEM((tm, d), BF16), pltpu.VMEM((tm, d), F32)],
        compiler_params=_params(("arbitrary", "arbitrary")),
        name=name,
    )(x, wg, wu, wd, g.reshape(1, d), b.reshape(1, d))


def _toeplitz(vec, rows, cols):
    hh, length = vec.shape
    assert length == rows + cols - 1
    padded = jnp.pad(vec, ((0, 0), (0, 1)))
    skew = jnp.tile(padded, (1, rows))[:, :rows * length].reshape(hh, rows, length)
    return skew[:, :, rows - 1:]


def _split_halves(q):
    lane = lax.broadcasted_iota(jnp.int32, q.shape, 1)
    half = q.shape[1] // 2
    zero = jnp.zeros_like(q)
    return jnp.concatenate([jnp.where(lane < half, q, zero), jnp.where(lane >= half, q, zero)], axis=0)


def _diff_finalize(o, lam_ref, g_ref, lam_init):
    t = o.shape[0] // 2
    lp = lam_ref[...]
    lam = (jnp.exp(jnp.sum(lp[0:1] * lp[1:2], axis=-1, keepdims=True))
           - jnp.exp(jnp.sum(lp[2:3] * lp[3:4], axis=-1, keepdims=True)) + lam_init)
    of = o[:t] - lam * o[t:]
    of = of * lax.rsqrt(jnp.mean(of * of, axis=-1, keepdims=True) + RMS_EPS) * g_ref[...]
    return of * (1.0 - lam_init)


def _neg_abs(z):
    return pltpu.bitcast(pltpu.bitcast(z, jnp.uint32) | jnp.uint32(0x80000000), F32)


def _suffix_matrix(n):
    return jnp.asarray(np.arange(n)[:, None] > np.arange(n)[None, :], BF16)


def _stick_tile(q, kt, vt, u, run, valid):
    z = lax.dot_general(q, kt, _NT, preferred_element_type=F32)
    log_beta = jnp.minimum(z, 0.0) - jnp.log(1.0 + jnp.exp2(_neg_abs(z))) * LOG2E
    log_1m = log_beta - z
    if valid is not None:
        log_1m = jnp.where(valid, log_1m, 0.0)
    suffix = jnp.dot(log_1m.astype(BF16), u, preferred_element_type=F32)
    a = jnp.exp2(log_beta + suffix + run)
    if valid is not None:
        a = jnp.where(valid, a, 0.0)
    return (jnp.dot(a.astype(BF16), vt, preferred_element_type=F32),
            run + jnp.sum(log_1m, axis=-1, keepdims=True))


def _attn_a_body(q_ref, k_ref, v_ref, bias_ref, o_ref, *, tq, nb, heads, hd):
    i = pl.program_id(2)
    starts = [pl.multiple_of(jnp.maximum(i - (nb - 1) + j, 0) * tq, tq) for j in range(nb)]
    for hh in range(heads):
        cols = slice(hh * hd, (hh + 1) * hd)
        q = q_ref[:, cols]
        scores = []
        for j in range(nb):
            s = lax.dot_general(q, k_ref[pl.ds(starts[j], tq), cols], _NT, preferred_element_type=F32)
            s = s + bias_ref[hh, :, j * tq:(j + 1) * tq]
            if j < nb - 1:
                s = jnp.where(i - (nb - 1) + j >= 0, s, NEG_INF)
            scores.append(s)
        m = functools.reduce(jnp.maximum, [jnp.max(s, axis=-1, keepdims=True) for s in scores])
        l = jnp.zeros_like(m)
        acc = jnp.zeros((tq, hd), F32)
        for start, s in zip(starts, scores):
            p = jnp.exp2(s - m)
            l = l + jnp.sum(p, axis=-1, keepdims=True)
            acc = acc + jnp.dot(p.astype(BF16), v_ref[pl.ds(start, tq), cols], preferred_element_type=F32)
        o_ref[:, cols] = (acc / l).astype(BF16)


def _clipped_bias_vec(table, rel):
    idx = np.clip(rel, -REL_CLIP_A, REL_CLIP_A) + REL_CLIP_A
    return jnp.transpose(table[idx]).astype(F32) * LOG2E


def _band_bias_prompt(table, tq, nb):
    w = nb * tq
    r = np.arange(tq)[:, None]
    off = np.arange(w)[None, :] - (nb - 1) * tq
    kc, qc = off // CHUNK, r // CHUNK
    valid = (kc <= qc) & (kc >= qc - BAND_CHUNKS)
    vec = _clipped_bias_vec(table, np.arange(tq + w - 1) - (tq - 1) - (nb - 1) * tq)
    return jnp.where(valid[None], _toeplitz(vec, tq, w), NEG_INF)


def _attn_a_prompt(qkv, table, b, t, h, hd, name):
    cfg = _tiles(b * t, t)
    tq, heads = cfg["tq_band"], cfg["band_heads"]
    assert BAND_PAST % tq == 0 and t % tq == 0 and tq % CHUNK == 0 and h % heads == 0
    nb = BAND_PAST // tq + 1
    nt = t // tq
    hg = h // heads
    bias = _band_bias_prompt(table, tq, nb)
    return pl.pallas_call(
        functools.partial(_attn_a_body, tq=tq, nb=nb, heads=heads, hd=hd),
        grid=(b, hg, nt),
        in_specs=[pl.BlockSpec((tq, heads * hd), lambda bi, gi, i: (bi * nt + i, gi)),
                  pl.BlockSpec((t, heads * hd), lambda bi, gi, i: (bi, hg + gi)),
                  pl.BlockSpec((t, heads * hd), lambda bi, gi, i: (bi, 2 * hg + gi)),
                  pl.BlockSpec((heads, tq, nb * tq), lambda bi, gi, i: (gi, 0, 0))],
        out_specs=pl.BlockSpec((tq, heads * hd), lambda bi, gi, i: (bi * nt + i, gi)),
        out_shape=jax.ShapeDtypeStruct((b * t, h * hd), BF16),
        compiler_params=_params(("arbitrary", "arbitrary", "arbitrary")),
        name=name,
    )(qkv, qkv, qkv, bias)


def _lane_fold(x, op):
    return functools.reduce(op, [x[:, c * LANES:(c + 1) * LANES] for c in range(x.shape[1] // LANES)])


STABILISER_SLACK = 64.0


def _attn_b_body(q_ref, k_ref, v_ref, near_ref, far_ref, lam_ref, g_ref, o_ref, acc_ref, stab_ref, l_ref,
                 track_ref, near_s_ref, *, tq, lam_init):
    i = pl.program_id(2)
    qq = _split_halves(q_ref[...])
    far = far_ref[:, 0:1]
    n_far = jnp.maximum(i - 1, 0)

    def far_scores(j):
        start = pl.multiple_of(j * tq, tq)
        return lax.dot_general(qq, k_ref[pl.ds(start, tq), :], _NT, preferred_element_type=F32), start

    def near_scores(j):
        s, start = far_scores(j)
        return (s.reshape(2, tq, tq) + near_ref[j - i + 1][None]).reshape(2 * tq, tq), start

    def near_max():
        track_ref[...] = jnp.full_like(track_ref, NEG_INF)

        def body(j, carry):
            s = near_scores(j)[0]
            near_s_ref[j - n_far] = s
            track_ref[...] = jnp.maximum(track_ref[...], _lane_fold(s, jnp.maximum))
            return carry

        lax.fori_loop(n_far, i + 1, body, 0)
        return jnp.max(track_ref[...], axis=-1, keepdims=True)

    def cached_near_scores(j):
        return near_s_ref[j - n_far], pl.multiple_of(j * tq, tq)

    def sum_pass(scores, shift, lo, hi, track):
        def body(j, carry):
            s, start = scores(j)
            if track:
                track_ref[...] = jnp.maximum(track_ref[...], _lane_fold(s, jnp.maximum))
            p = jnp.exp2(s - jnp.tile(stab_ref[...] - shift, (1, tq // LANES)))
            l_ref[...] += _lane_fold(p, jnp.add)
            acc_ref[...] += jnp.dot(p.astype(BF16), v_ref[pl.ds(start, tq), :], preferred_element_type=F32)
            return carry

        lax.fori_loop(lo, hi, body, 0)

    def sum_all(m, track):
        stab_ref[...] = jnp.broadcast_to(m, stab_ref.shape)
        l_ref[...] = jnp.zeros_like(l_ref)
        acc_ref[...] = jnp.zeros_like(acc_ref)
        sum_pass(far_scores, far, 0, n_far, track)
        sum_pass(cached_near_scores, 0.0, n_far, i + 1, False)

    def finish():
        l = jnp.sum(l_ref[...], axis=-1, keepdims=True)
        o_ref[...] = _diff_finalize(acc_ref[...] / l, lam_ref, g_ref, lam_init).astype(BF16)

    m_near = near_max()
    track_ref[...] = jnp.full_like(track_ref, NEG_INF)
    sum_all(m_near, True)
    m_far = jnp.max(track_ref[...], axis=-1, keepdims=True) + far
    within_slack = jnp.max(m_far - m_near) <= STABILISER_SLACK

    @pl.when(within_slack)
    def _():
        finish()

    @pl.when(jnp.logical_not(within_slack))
    def _():
        sum_all(jnp.maximum(m_far, m_near), False)
        finish()


def _t5_bucket(rel):
    half = T5_BUCKETS // 2
    max_exact = half // 2
    n = jnp.abs(rel)
    nf = jnp.maximum(n, 1).astype(F32)
    large = max_exact + (jnp.log(nf / max_exact) / math.log(T5_MAX_DIST / max_exact)
                         * (half - max_exact)).astype(jnp.int32)
    return jnp.where(rel > 0, half, 0) + jnp.where(n < max_exact, n, jnp.minimum(large, half - 1))


def _t5_far_distance():
    half = T5_BUCKETS // 2
    max_exact = half // 2
    return math.ceil(max_exact * (T5_MAX_DIST / max_exact) ** ((half - 1 - max_exact + 0.5) / (half - max_exact)))


def _t5_bias_vec(table, rel):
    return jnp.transpose(table[_t5_bucket(jnp.asarray(rel, jnp.int32))]).astype(F32) * LOG2E


def _t5_bias_prompt(table, tq):
    assert tq + 1 >= _t5_far_distance()
    r = np.arange(tq)[:, None]
    c = np.arange(tq)[None, :]
    left = _toeplitz(_t5_bias_vec(table, np.arange(2 * tq - 1) - (tq - 1) - tq), tq, tq)
    diag = _toeplitz(_t5_bias_vec(table, np.arange(2 * tq - 1) - (tq - 1)), tq, tq)
    diag = jnp.where(((c // CHUNK) <= (r // CHUNK))[None], diag, NEG_INF)
    far = jnp.broadcast_to(_t5_bias_vec(table, np.array([-(tq + 1)]))[:, :, None], (table.shape[1], 1, LANES))
    return jnp.stack([left, diag], axis=1), far


def _attn_b_prompt(qkv, table, lam_rows, gain, lam_init, b, t, h, hd, name):
    tq = _tiles(b * t, t)["tq_diff"]
    assert t % tq == 0 and tq % CHUNK == 0
    nt = t // tq
    near, far = _t5_bias_prompt(table, tq)
    return pl.pallas_call(
        functools.partial(_attn_b_body, tq=tq, lam_init=lam_init),
        grid=(b, h, nt),
        in_specs=[pl.BlockSpec((tq, hd), lambda bi, hi, i: (bi * nt + i, hi)),
                  pl.BlockSpec((t, hd), lambda bi, hi, i: (bi, h + hi)),
                  pl.BlockSpec((t, hd), lambda bi, hi, i: (bi, 2 * h + hi)),
                  pl.BlockSpec((None, 2, tq, tq), lambda bi, hi, i: (hi, 0, 0, 0)),
                  pl.BlockSpec((None, 1, LANES), lambda bi, hi, i: (hi, 0, 0)),
                  pl.BlockSpec(lam_rows.shape, lambda bi, hi, i: (0, 0)),
                  pl.BlockSpec((1, hd), lambda bi, hi, i: (0, 0))],
        out_specs=pl.BlockSpec((tq, hd), lambda bi, hi, i: (bi * nt + i, hi)),
        out_shape=jax.ShapeDtypeStruct((b * t, h * hd), BF16),
        scratch_shapes=([pltpu.VMEM((2 * tq, hd), F32)] + [pltpu.VMEM((2 * tq, LANES), F32)] * 3
                        + [pltpu.VMEM((2, 2 * tq, tq), F32)]),
        compiler_params=_params(("arbitrary", "arbitrary", "arbitrary")),
        name=name,
    )(qkv, qkv, qkv, near, far, lam_rows, gain.reshape(1, hd))


def _attn_c_body(q_ref, k_ref, v_ref, u_ref, o_ref, acc_ref, *, tq, tk):
    i = pl.program_id(2)
    q = q_ref[...]
    u = u_ref[...]
    subs = tq // tk
    below_diag = (lax.broadcasted_iota(jnp.int32, (tk, tk), 1) < lax.broadcasted_iota(jnp.int32, (tk, tk), 0))

    def group(base, run):
        total = None
        for sub in reversed(range(subs)):
            start = pl.multiple_of(base + sub * tk, tk)
            pv, run = _stick_tile(q, k_ref[pl.ds(start, tk), :], v_ref[pl.ds(start, tk), :], u, run, None)
            total = pv if total is None else total + pv
        return total, run

    runs = []
    for rb in range(subs):
        rows = slice(rb * tk, (rb + 1) * tk)
        total, run = None, jnp.zeros((tk, 1), F32)
        for sub in reversed(range(rb + 1)):
            start = pl.multiple_of(i * tq + sub * tk, tk)
            pv, run = _stick_tile(q[rows], k_ref[pl.ds(start, tk), :], v_ref[pl.ds(start, tk), :], u, run,
                                  below_diag if sub == rb else None)
            total = pv if total is None else total + pv
        acc_ref[rows, :] = total
        runs.append(run)
    run = jnp.concatenate(runs, axis=0)

    def one_group(base, run):
        pv, run = group(base, run)
        acc_ref[...] += pv
        return run

    run = lax.cond(i % 2 == 1, lambda r: one_group((i - 1) * tq, r), lambda r: r, run)
    first = i - 1 - i % 2

    def body(step, run):
        pv0, run = group((first - 2 * step) * tq, run)
        pv1, run = group((first - 2 * step - 1) * tq, run)
        acc_ref[...] += pv0 + pv1
        return run

    lax.fori_loop(0, i // 2, body, run)
    o_ref[...] = acc_ref[...].astype(BF16)


def _attn_c_prompt(qkv, b, t, h, hd, name):
    cfg = _tiles(b * t, t)
    tq, tk = cfg["tq_stick"], min(cfg["tk_stick"], cfg["tq_stick"])
    assert t % tq == 0 and tq % tk == 0
    nt = t // tq
    return pl.pallas_call(
        functools.partial(_attn_c_body, tq=tq, tk=tk),
        grid=(b, h, nt),
        in_specs=[pl.BlockSpec((tq, hd), lambda bi, hi, i: (bi * nt + i, hi)),
                  pl.BlockSpec((t, hd), lambda bi, hi, i: (bi, h + hi)),
                  pl.BlockSpec((t, hd), lambda bi, hi, i: (bi, 2 * h + hi)),
                  pl.BlockSpec((tk, tk), lambda bi, hi, i: (0, 0))],
        out_specs=pl.BlockSpec((tq, hd), lambda bi, hi, i: (bi * nt + i, hi)),
        out_shape=jax.ShapeDtypeStruct((b * t, h * hd), BF16),
        scratch_shapes=[pltpu.VMEM((tq, hd), F32)],
        compiler_params=_params(("arbitrary", "arbitrary", "arbitrary")),
        name=name,
    )(qkv, qkv, qkv, _suffix_matrix(tk))


def _samp_softmax_body(q_ref, kn_ref, vn_ref, kc_ref, vc_ref, bc_ref, bn_ref, *rest, hd, diff, lam_init):
    o_ref = rest[-1]
    t = q_ref.shape[0]
    reps = 2 if diff else 1
    for hh in range(SUBLANES):
        cols = slice(hh * hd, (hh + 1) * hd)
        q = q_ref[:, cols]
        qq = _split_halves(q) if diff else q

        def scores(k, bias):
            s = lax.dot_general(qq, k, _NT, preferred_element_type=F32)
            return (s.reshape(reps, t, -1) + bias[None]).reshape(reps * t, -1)

        s_c = scores(kc_ref[:, hh, :].astype(BF16), bc_ref[hh])
        s_n = scores(kn_ref[:, cols], bn_ref[hh])
        m = jnp.maximum(jnp.max(s_c, axis=-1, keepdims=True), jnp.max(s_n, axis=-1, keepdims=True))
        p_c = jnp.exp2(s_c - m)
        p_n = jnp.exp2(s_n - m)
        l = jnp.sum(p_c, axis=-1, keepdims=True) + jnp.sum(p_n, axis=-1, keepdims=True)
        acc = (jnp.dot(p_c.astype(BF16), vc_ref[:, hh, :].astype(BF16), preferred_element_type=F32)
               + jnp.dot(p_n.astype(BF16), vn_ref[:, cols], preferred_element_type=F32))
        o = acc / l
        if diff:
            o = _diff_finalize(o, rest[0], rest[1], lam_init)
        o_ref[:, cols] = o.astype(BF16)


def _sample_specs(cache_k, cache_v, layer, bs, t, h, hd):
    assert h % SUBLANES == 0
    groups = h // SUBLANES
    width = SUBLANES * hd
    r = cache_k.shape[2]
    ck = cache_k.reshape(cache_k.shape[0], bs, r, groups, SUBLANES, hd)
    cv = cache_v.reshape(cache_v.shape[0], bs, r, groups, SUBLANES, hd)
    cache_spec = pl.BlockSpec((None, None, r, None, SUBLANES, hd), lambda bi, gi: (layer, bi, 0, gi, 0, 0))
    specs = [pl.BlockSpec((t, width), lambda bi, gi: (bi, gi)),
             pl.BlockSpec((t, width), lambda bi, gi: (bi, groups + gi)),
             pl.BlockSpec((t, width), lambda bi, gi: (bi, 2 * groups + gi)),
             cache_spec, cache_spec]
    out_spec = pl.BlockSpec((t, width), lambda bi, gi: (bi, gi))
    return (bs, groups), specs, out_spec, ck, cv


def _attn_softmax_sample(qkv, cache_k, cache_v, layer, bias, bs, t, h, hd, name, diff_args=None):
    r = cache_k.shape[2]
    grid, in_specs, out_spec, ck, cv = _sample_specs(cache_k, cache_v, layer, bs, t, h, hd)
    in_specs += [pl.BlockSpec((SUBLANES, t, r), lambda bi, gi: (gi, 0, 0)),
                 pl.BlockSpec((SUBLANES, t, t), lambda bi, gi: (gi, 0, 0))]
    args = [qkv, qkv, qkv, ck, cv, bias[:, :, :r], bias[:, :, r:]]
    lam_init = None
    if diff_args is not None:
        lam_rows, gain, lam_init = diff_args
        in_specs += [pl.BlockSpec(lam_rows.shape, lambda bi, gi: (0, 0)),
                     pl.BlockSpec((1, hd), lambda bi, gi: (0, 0))]
        args += [lam_rows, gain.reshape(1, hd)]
    return pl.pallas_call(
        functools.partial(_samp_softmax_body, hd=hd, diff=diff_args is not None, lam_init=lam_init),
        grid=grid,
        in_specs=in_specs,
        out_specs=out_spec,
        out_shape=jax.ShapeDtypeStruct((bs * t, h * hd), BF16),
        compiler_params=_params(("arbitrary", "arbitrary")),
        name=name,
    )(*args)


def _samp_stick_body(q_ref, kn_ref, vn_ref, kc_ref, vc_ref, un_ref, u_ref, o_ref, *, tk, hd):
    t = q_ref.shape[0]
    valid = lax.broadcasted_iota(jnp.int32, (t, t), 1) < lax.broadcasted_iota(jnp.int32, (t, t), 0)
    for hh in range(SUBLANES):
        cols = slice(hh * hd, (hh + 1) * hd)
        q = q_ref[:, cols]
        acc, run = _stick_tile(q, kn_ref[:, cols], vn_ref[:, cols], un_ref[...], jnp.zeros((t, 1), F32), valid)
        for tile in reversed(range(kc_ref.shape[0] // tk)):
            rows = slice(tile * tk, (tile + 1) * tk)
            pv, run = _stick_tile(q, kc_ref[rows, hh, :].astype(BF16), vc_ref[rows, hh, :].astype(BF16),
                                  u_ref[...], run, None)
            acc = acc + pv
        o_ref[:, cols] = acc.astype(BF16)


def _attn_stick_sample(qkv, cache_k, cache_v, layer, bs, t, h, hd, name):
    r = cache_k.shape[2]
    tk = MXU_DIM if r % MXU_DIM == 0 else LANES
    assert r % tk == 0
    grid, in_specs, out_spec, ck, cv = _sample_specs(cache_k, cache_v, layer, bs, t, h, hd)
    in_specs += [pl.BlockSpec((t, t), lambda bi, gi: (0, 0)),
                 pl.BlockSpec((tk, tk), lambda bi, gi: (0, 0))]
    return pl.pallas_call(
        functools.partial(_samp_stick_body, tk=tk, hd=hd),
        grid=grid,
        in_specs=in_specs,
        out_specs=out_spec,
        out_shape=jax.ShapeDtypeStruct((bs * t, h * hd), BF16),
        compiler_params=_params(("arbitrary", "arbitrary")),
        name=name,
    )(qkv, qkv, qkv, ck, cv, _suffix_matrix(t), _suffix_matrix(tk))


def _band_bias_sample(table, past_len, r, t):
    qpos = past_len + np.arange(t)
    kpos = past_len - r + np.arange(r + t)
    qc, kc = qpos[:, None] // CHUNK, kpos[None, :] // CHUNK
    valid = (kc <= qc) & (kc >= qc - BAND_CHUNKS)
    vec = _clipped_bias_vec(table, np.arange(r + 2 * t - 1) - (t - 1) + (kpos[0] - qpos[0]))
    return jnp.where(valid[None], _toeplitz(vec, t, r + t), NEG_INF)


def _t5_bias_sample(table, r, t):
    qpos = r + np.arange(t)
    kpos = np.arange(r + t)
    valid = (kpos[None, :] // CHUNK) <= (qpos[:, None] // CHUNK)
    vec = _t5_bias_vec(table, np.arange(r + 2 * t - 1) - (t - 1) + (kpos[0] - qpos[0]))
    return jnp.where(valid[None], _toeplitz(vec, t, r + t), NEG_INF)


def kernel(x_prompt, x_sample, cache_a_k, cache_a_v, cache_b_k, cache_b_v, cache_c_k, cache_c_v, w_in_a, w_out_a, rel_bias_a, w_in_b, w_out_b, lambda_q1, lambda_k1, lambda_q2, lambda_k2, diff_norm_g, t5_bias, w_in_c, w_out_c, ln1_g, ln1_b, ln2_g, ln2_b, w_gate, w_up, w_down):
    b, t, d = x_prompt.shape
    bs, ts, _ = x_sample.shape
    h, hd = cache_a_k.shape[3], cache_a_k.shape[4]
    depth = ln1_g.shape[0]
    alpha = (2 * depth) ** 0.25
    past_len = cache_b_k.shape[2]

    yp = x_prompt.reshape(b * t, d)
    ys = x_sample.reshape(bs * ts, d)
    yp_b, ys_b = yp.astype(BF16), ys.astype(BF16)
    new = {key: [] for key in ("pa_k", "pa_v", "pb_k", "pb_v", "pc_k", "pc_v",
                               "sa_k", "sa_v", "sb_k", "sb_v", "sc_k", "sc_v")}
    for i in range(depth):
        kind, j = i % N_MIXERS, i // N_MIXERS
        w_in = (w_in_a, w_in_b, w_in_c)[kind][j].astype(BF16)
        w_out = (w_out_a, w_out_b, w_out_c)[kind][j].astype(BF16)
        q_scale = ((hd // 2) ** -0.5 if kind == 1 else hd ** -0.5) * LOG2E
        qkv_p, kp, vp = _qkv_proj(yp_b, w_in, q_scale, hd, kind != 0, f"qkv_prompt_{i}")
        qkv_s, ks, vs = _qkv_proj(ys_b, w_in, q_scale, hd, True, f"qkv_sample_{i}")
        kp, vp = kp.reshape(b, t, h, hd), vp.reshape(b, t, h, hd)
        ks, vs = ks.reshape(bs, ts, h, hd), vs.reshape(bs, ts, h, hd)
        if kind == 0:
            op = _attn_a_prompt(qkv_p, rel_bias_a[j], b, t, h, hd, f"band_prompt_{i}")
            r = cache_a_k.shape[2]
            os_ = _attn_softmax_sample(qkv_s, cache_a_k, cache_a_v, j,
                                       _band_bias_sample(rel_bias_a[j], past_len, r, ts),
                                       bs, ts, h, hd, f"band_sample_{i}")
            keep = min(BAND_PAST, t)
            kp, vp = kp[:, -keep:], vp[:, -keep:]
            tag = "a"
        elif kind == 1:
            lam_init = 0.8 - 0.6 * math.exp(-0.3 * i)
            lam_rows = jnp.stack([lambda_q1[j], lambda_k1[j], lambda_q2[j], lambda_k2[j]]).astype(F32)
            op = _attn_b_prompt(qkv_p, t5_bias, lam_rows, diff_norm_g[j], lam_init, b, t, h, hd,
                                f"diff_prompt_{i}")
            r = cache_b_k.shape[2]
            os_ = _attn_softmax_sample(qkv_s, cache_b_k, cache_b_v, j, _t5_bias_sample(t5_bias, r, ts),
                                       bs, ts, h, hd, f"diff_sample_{i}",
                                       diff_args=(lam_rows, diff_norm_g[j], lam_init))
            tag = "b"
        else:
            op = _attn_c_prompt(qkv_p, b, t, h, hd, f"stick_prompt_{i}")
            os_ = _attn_stick_sample(qkv_s, cache_c_k, cache_c_v, j, bs, ts, h, hd, f"stick_sample_{i}")
            tag = "c"
        new[f"p{tag}_k"].append(kp)
        new[f"p{tag}_v"].append(vp)
        new[f"s{tag}_k"].append(ks)
        new[f"s{tag}_v"].append(vs)
        yp = _outproj_ln(op, yp, w_out, ln1_g[i], ln1_b[i], alpha, f"outproj_prompt_{i}")
        ys = _outproj_ln(os_, ys, w_out, ln1_g[i], ln1_b[i], alpha, f"outproj_sample_{i}")
        wg, wu, wd = w_gate[i].astype(BF16), w_up[i].astype(BF16), w_down[i].astype(BF16)
        yp, yp_b = _ffn_ln(yp, wg, wu, wd, ln2_g[i], ln2_b[i], alpha, f"ffn_prompt_{i}")
        ys, ys_b = _ffn_ln(ys, wg, wu, wd, ln2_g[i], ln2_b[i], alpha, f"ffn_sample_{i}")
    return (yp.reshape(b, t, d), ys.reshape(bs, ts, d),
            jnp.stack(new["pa_k"]), jnp.stack(new["pa_v"]), jnp.stack(new["pb_k"]), jnp.stack(new["pb_v"]),
            jnp.stack(new["pc_k"]), jnp.stack(new["pc_v"]), jnp.stack(new["sa_k"]), jnp.stack(new["sa_v"]),
            jnp.stack(new["sb_k"]), jnp.stack(new["sb_v"]), jnp.stack(new["sc_k"]), jnp.stack(new["sc_v"]))
```
